```python
import jax, jax.numpy as jnp
from jax import lax
import numpy as np

D_MODEL = 1024
BATCH = 4
SEQ = 4096
DEPTH = 1

N_MEM = 256
ATTN_HEADS = 8
ATTN_HEAD_DIM = D_MODEL // ATTN_HEADS
ATTN_WIDTH = ATTN_HEADS * ATTN_HEAD_DIM
MOBA_BLOCK = 256
MOBA_TOPK = 3
Q_CHUNK = 128
CONV_CH = D_MODEL
CONV_WIDTH = 31
XATTN_HEADS = 4
XATTN_HEAD_DIM = D_MODEL // XATTN_HEADS
PEER_HEADS = 8
PEER_QDIM = 256
PEER_NKEYS = 128
PEER_EXPERTS = PEER_NKEYS * PEER_NKEYS
PEER_TOPK = 16
PEER_CHUNK = 128
IN_WIDTH = 3 * ATTN_WIDTH + 2 * CONV_CH + 2 * D_MODEL
EPS = 1e-6

kernel_name = 'hybrid_conformer_moba_peer_block'


def rms_norm(x, g):
    xf = x.astype(jnp.float32)
    y = xf * lax.rsqrt(jnp.mean(xf * xf, axis=-1, keepdims=True) + EPS) * g.astype(jnp.float32)
    return y.astype(x.dtype)


def layer_norm(x, g, b):
    xf = x.astype(jnp.float32)
    mu = jnp.mean(xf, axis=-1, keepdims=True)
    var = jnp.mean(jnp.square(xf - mu), axis=-1, keepdims=True)
    y = (xf - mu) * lax.rsqrt(var + EPS) * g.astype(jnp.float32) + b.astype(jnp.float32)
    return y.astype(x.dtype)


def alibi_slopes(n):
    return 2.0 ** (-8.0 * jnp.arange(1, n + 1, dtype=jnp.float32) / n)


def split_heads(t, n_heads):
    B, S, W = t.shape
    return t.reshape(B, S, n_heads, W // n_heads).transpose(0, 2, 1, 3)


def conformer_conv(a, b, dw_w, dw_b, ln_g, ln_b, w_out, b_out):
    u = a * jax.nn.sigmoid(b)
    C = u.shape[-1]
    u = lax.conv_general_dilated(u, dw_w[:, None, :].astype(u.dtype), window_strides=(1,),
                                 padding=[(CONV_WIDTH - 1, 0)],
                                 dimension_numbers=('NWC', 'WIO', 'NWC'),
                                 feature_group_count=C) + dw_b
    u = jax.nn.silu(layer_norm(u, ln_g, ln_b))
    return u @ w_out + b_out


def moba_attention(q, k, v):
    B, H, S, hd = q.shape
    L = MOBA_BLOCK
    nb = -(-S // L)
    pad = nb * L - S
    if pad:
        k = jnp.pad(k, ((0, 0), (0, 0), (0, pad), (0, 0)))
        v = jnp.pad(v, ((0, 0), (0, 0), (0, pad), (0, 0)))
    k_blocks = k.reshape(B, H, nb, L, hd)
    v_blocks = v.reshape(B, H, nb, L, hd)
    k_mean = jnp.mean(k_blocks.astype(jnp.float32), axis=3)
    n_sel = min(MOBA_TOPK, nb)
    slopes = alibi_slopes(H)[None, :, None, None]
    scale = hd ** -0.5
    b_idx = jnp.arange(B)[:, None, None]
    h_idx = jnp.arange(H)[None, :, None]
    offs = jnp.arange(L)

    def chunk(c):
        start = c * Q_CHUNK
        blk = start // L
        qc = lax.dynamic_slice_in_dim(q, start, Q_CHUNK, axis=2)
        t = (start + jnp.arange(Q_CHUNK))[None, None, :, None]
        gate = jnp.einsum('bhqd,bhnd->bhqn', qc.astype(jnp.float32), k_mean)
        gate = jnp.where(jnp.arange(nb) < blk, gate, -jnp.inf)
        _, sel = lax.top_k(gate, n_sel)
        logits = []
        for j in range(n_sel):
            idx = sel[..., j]
            kj = k_blocks[b_idx, h_idx, idx]
            s = jnp.einsum('bhqd,bhqld->bhql', qc, kj).astype(jnp.float32) * scale
            pos = idx[..., None] * L + offs
            s = s - slopes * (t - pos).astype(jnp.float32)
            logits.append(jnp.where(j < blk, s, -jnp.inf))
        k_own = lax.dynamic_slice_in_dim(k, blk * L, L, axis=2)
        v_own = lax.dynamic_slice_in_dim(v, blk * L, L, axis=2)
        s = jnp.einsum('bhqd,bhld->bhql', qc, k_own).astype(jnp.float32) * scale
        pos = blk * L + offs
        s = s - slopes * (t - pos).astype(jnp.float32)
        logits.append(jnp.where(pos <= t, s, -jnp.inf))
        p = jax.nn.softmax(jnp.concatenate(logits, axis=-1), axis=-1).astype(v.dtype)
        out = jnp.einsum('bhql,bhld->bhqd', p[..., n_sel * L:], v_own)
        for j in range(n_sel):
            vj = v_blocks[b_idx, h_idx, sel[..., j]]
            out = out + jnp.einsum('bhql,bhqld->bhqd', p[..., j * L:(j + 1) * L], vj)
        return out

    out = lax.map(chunk, jnp.arange(S // Q_CHUNK))
    return out.transpose(1, 0, 3, 2, 4).reshape(B, S, H * hd)


def memory_cross_attention(hn, mn, w_q, w_kv, w_o):
    B, S, D = hn.shape
    M = mn.shape[1]
    q = (hn @ w_q).reshape(B, S, XATTN_HEADS, XATTN_HEAD_DIM)
    kv = (mn @ w_kv).reshape(B, M, 2, XATTN_HEADS, XATTN_HEAD_DIM)
    k, v = kv[:, :, 0], kv[:, :, 1]
    s = jnp.einsum('bshd,bmhd->bhsm', q, k).astype(jnp.float32) * (XATTN_HEAD_DIM ** -0.5)
    p = jax.nn.softmax(s, axis=-1).astype(v.dtype)
    o = jnp.einsum('bhsm,bmhd->bshd', p, v).reshape(B, S, D)
    return o @ w_o


def peer_ffn(xn, w_q, subkeys, u_tab, v_tab):
    B, S, D = xn.shape
    nc = S // PEER_CHUNK
    half = PEER_QDIM // 2
    xc = xn.reshape(B, nc, PEER_CHUNK, D).transpose(1, 0, 2, 3)

    def chunk(xb):
        q = (xb @ w_q).reshape(B, PEER_CHUNK, PEER_HEADS, PEER_QDIM)
        s1 = jnp.einsum('bqhd,hnd->bqhn', q[..., :half], subkeys[:, 0]).astype(jnp.float32)
        s2 = jnp.einsum('bqhd,hnd->bqhn', q[..., half:], subkeys[:, 1]).astype(jnp.float32)
        v1, i1 = lax.top_k(s1, PEER_TOPK)
        v2, i2 = lax.top_k(s2, PEER_TOPK)
        cand = (v1[..., :, None] + v2[..., None, :]).reshape(B, PEER_CHUNK, PEER_HEADS, PEER_TOPK * PEER_TOPK)
        sc, ci = lax.top_k(cand, PEER_TOPK)
        e = (jnp.take_along_axis(i1, ci // PEER_TOPK, axis=-1) * PEER_NKEYS
             + jnp.take_along_axis(i2, ci % PEER_TOPK, axis=-1))
        g = jax.nn.softmax(sc, axis=-1)
        ue = u_tab[e]
        a = jax.nn.gelu(jnp.einsum('bqhkd,bqd->bqhk', ue, xb).astype(jnp.float32), approximate=False)
        ve = v_tab[e]
        return jnp.einsum('bqhk,bqhkd->bqd', (g * a).astype(ve.dtype), ve)

    out = lax.map(chunk, xc)
    return out.transpose(1, 0, 2, 3).reshape(B, S, D)


def setup_inputs(seed: int = 0) -> dict:
    key = jax.random.key(seed)
    ks = jax.random.split(key, 24)
    L = DEPTH

    def nrm(k, shape, scale):
        return jax.random.normal(k, shape, jnp.float32) * scale

    def gain(k, shape):
        return 1.0 + 0.05 * jax.random.normal(k, shape, jnp.float32)

    return {
        'x': nrm(ks[0], (BATCH, SEQ, D_MODEL), 1.0),
        'mem': nrm(ks[1], (BATCH, N_MEM, D_MODEL), 1.0),
        'mix_norm_g': gain(ks[2], (L, D_MODEL)),
        'w_in': nrm(ks[3], (L, D_MODEL, IN_WIDTH), D_MODEL ** -0.5),
        'conv_dw_w': nrm(ks[4], (L, CONV_WIDTH, CONV_CH), CONV_WIDTH ** -0.5),
        'conv_dw_b': nrm(ks[5], (L, CONV_CH), 0.02),
        'conv_ln_g': gain(ks[6], (L, CONV_CH)),
        'conv_ln_b': nrm(ks[7], (L, CONV_CH), 0.02),
        'w_conv_out': nrm(ks[8], (L, CONV_CH, D_MODEL), CONV_CH ** -0.5),
        'b_conv_out': nrm(ks[9], (L, D_MODEL), 0.02),
        'w_attn_out': nrm(ks[10], (L, ATTN_WIDTH, D_MODEL), ATTN_WIDTH ** -0.5),
        'w_mix_out': nrm(ks[11], (L, D_MODEL, D_MODEL), D_MODEL ** -0.5),
        'xattn_norm_g': gain(ks[12], (L, D_MODEL)),
        'mem_norm_g': gain(ks[13], (L, D_MODEL)),
        'w_xq': nrm(ks[14], (L, D_MODEL, D_MODEL), D_MODEL ** -0.5),
        'w_xkv': nrm(ks[15], (L, D_MODEL, 2 * D_MODEL), D_MODEL ** -0.5),
        'w_xo': nrm(ks[16], (L, D_MODEL, D_MODEL), D_MODEL ** -0.5),
        'ffn_norm_g': gain(ks[17], (L, D_MODEL)),
        'w_peer_q': nrm(ks[18], (L, D_MODEL, PEER_HEADS * PEER_QDIM), D_MODEL ** -0.5),
        'peer_subkeys': nrm(ks[19], (L, PEER_HEADS, 2, PEER_NKEYS, PEER_QDIM // 2), (PEER_QDIM // 2) ** -0.5),
        'peer_u': nrm(ks[20], (L, PEER_EXPERTS, D_MODEL), D_MODEL ** -0.5),
        'peer_v': nrm(ks[21], (L, PEER_EXPERTS, D_MODEL), PEER_HEADS ** -0.5),
        'final_norm_g': gain(ks[22], (D_MODEL,)),
    }


def reference(x, mem, mix_norm_g, w_in, conv_dw_w, conv_dw_b, conv_ln_g, conv_ln_b, w_conv_out, b_conv_out,
              w_attn_out, w_mix_out, xattn_norm_g, mem_norm_g, w_xq, w_xkv, w_xo, ffn_norm_g, w_peer_q,
              peer_subkeys, peer_u, peer_v, final_norm_g):
    A, C, D = ATTN_WIDTH, CONV_CH, D_MODEL
    cuts = [A, 2 * A, 3 * A, 3 * A + C, 3 * A + 2 * C, 3 * A + 2 * C + D]
    h = x
    for l in range(DEPTH):
        xn = rms_norm(h, mix_norm_g[l])
        proj = xn @ w_in[l]
        q, k, v, glu_a, glu_b, gate_c, gate_a = jnp.split(proj, cuts, axis=-1)
        conv_out = conformer_conv(glu_a, glu_b, conv_dw_w[l], conv_dw_b[l], conv_ln_g[l], conv_ln_b[l],
                                  w_conv_out[l], b_conv_out[l])
        attn = moba_attention(split_heads(q, ATTN_HEADS), split_heads(k, ATTN_HEADS),
                              split_heads(v, ATTN_HEADS)) @ w_attn_out[l]
        merged = jax.nn.sigmoid(gate_c) * conv_out + jax.nn.sigmoid(gate_a) * attn
        h = h + merged @ w_mix_out[l]
        h = h + memory_cross_attention(rms_norm(h, xattn_norm_g[l]), rms_norm(mem, mem_norm_g[l]),
                                       w_xq[l], w_xkv[l], w_xo[l])
        h = h + peer_ffn(rms_norm(h, ffn_norm_g[l]), w_peer_q[l], peer_subkeys[l], peer_u[l], peer_v[l])
    return rms_norm(h, final_norm_g)
```

```python
import functools
import math

import jax
import jax.numpy as jnp
from jax import lax
from jax.experimental import pallas as pl
from jax.experimental.pallas import tpu as pltpu

F32 = jnp.float32
BF16 = jnp.bfloat16

EPS = 1e-6
ATTN_HEADS = 8
MOBA_BLOCK = 256
MOBA_TOPK = 3
CONV_WIDTH = 31
XATTN_HEADS = 4
PEER_HEADS = 8
PEER_NKEYS = 128
PEER_TOPK = 16
NEG = -1e30

LANES = 128
SUBLANES = 8
VMEM_LIMIT = 56 * 1024 * 1024
CONV_HALO = 32


def _params(*sem):
    return pltpu.CompilerParams(dimension_semantics=sem, vmem_limit_bytes=VMEM_LIMIT)


def _dot_nt(a, b):
    return lax.dot_general(a, b, (((1,), (1,)), ((), ())), preferred_element_type=F32)


def _rms(x, g):
    return x * lax.rsqrt(jnp.mean(x * x, axis=-1, keepdims=True) + EPS) * g


def _rms_matmul_kernel(x_ref, g_ref, w_ref, o_ref, xn_ref):
    @pl.when(pl.program_id(1) == 0)
    def _():
        xn_ref[...] = _rms(x_ref[...], g_ref[...]).astype(BF16)

    o_ref[...] = jnp.dot(xn_ref[...], w_ref[...], preferred_element_type=F32).astype(o_ref.dtype)


def _rms_matmul(x, g, w, tm, tn):
    T, D = x.shape
    N = w.shape[1]
    return pl.pallas_call(
        _rms_matmul_kernel,
        grid=(T // tm, N // tn),
        in_specs=[pl.BlockSpec((tm, D), lambda i, j: (i, 0)),
                  pl.BlockSpec((1, D), lambda i, j: (0, 0)),
                  pl.BlockSpec((D, tn), lambda i, j: (0, j))],
        out_specs=pl.BlockSpec((tm, tn), lambda i, j: (i, j)),
        out_shape=jax.ShapeDtypeStruct((T, N), BF16),
        scratch_shapes=[pltpu.VMEM((tm, D), BF16)],
        compiler_params=_params("parallel", "arbitrary"),
        name="rms_in_proj",
    )(x, g.reshape(1, D), w)


def _conv_kernel(a_ref, b_ref, gc_ref, dww_ref, dwb_ref, lng_ref, lnb_ref, w_ref, bo_ref, o_ref,
                 uext_ref, act_ref, *, ts, rc):
    C = a_ref.shape[-1]
    half = C // 2

    @pl.when(pl.program_id(1) == 0)
    def _():
        uext_ref[0:CONV_HALO, :] = jnp.zeros((CONV_HALO, C), F32)

    uext_ref[CONV_HALO:CONV_HALO + ts, :] = a_ref[0].astype(F32) * jax.nn.sigmoid(b_ref[0].astype(F32))

    first = CONV_HALO - (CONV_WIDTH - 1)

    def row_chunk(r, carry):
        r0 = pl.multiple_of(r * rc, rc)
        parts = []
        for c0 in (0, half):
            win = uext_ref[pl.ds(r0, rc + CONV_HALO), c0:c0 + half]
            acc = jnp.broadcast_to(dwb_ref[:, c0:c0 + half], (rc, half))
            for w in range(CONV_WIDTH):
                acc = acc + dww_ref[w:w + 1, c0:c0 + half] * win[first + w:first + w + rc, :]
            parts.append(acc)
        y = jnp.concatenate(parts, axis=1)
        mu = jnp.mean(y, axis=-1, keepdims=True)
        yc = y - mu
        var = jnp.mean(yc * yc, axis=-1, keepdims=True)
        z = yc * lax.rsqrt(var + EPS) * lng_ref[...] + lnb_ref[...]
        act_ref[pl.ds(r0, rc), :] = (z * jax.nn.sigmoid(z)).astype(BF16)
        return carry

    lax.fori_loop(0, ts // rc, row_chunk, 0)

    uext_ref[0:CONV_HALO, :] = uext_ref[ts:ts + CONV_HALO, :]
    co = jnp.dot(act_ref[...], w_ref[...], preferred_element_type=F32) + bo_ref[...]
    o_ref[0] = (jax.nn.sigmoid(gc_ref[0].astype(F32)) * co).astype(o_ref.dtype)


def _conv_branch(proj, dw_w, dw_b, ln_g, ln_b, w_out, b_out, ts, col_a, col_b, col_g):
    B, S, _ = proj.shape
    C = w_out.shape[0]
    D = w_out.shape[1]
    vec = lambda c: pl.BlockSpec((1, c), lambda b, s: (0, 0))
    return pl.pallas_call(
        functools.partial(_conv_kernel, ts=ts, rc=16),
        grid=(B, S // ts),
        in_specs=[pl.BlockSpec((1, ts, C), lambda b, s: (b, s, col_a)),
                  pl.BlockSpec((1, ts, C), lambda b, s: (b, s, col_b)),
                  pl.BlockSpec((1, ts, D), lambda b, s: (b, s, col_g)),
                  pl.BlockSpec((CONV_WIDTH, C), lambda b, s: (0, 0)),
                  vec(C), vec(C), vec(C),
                  pl.BlockSpec((C, D), lambda b, s: (0, 0)),
                  vec(D)],
        out_specs=pl.BlockSpec((1, ts, D), lambda b, s: (b, s, 0)),
        out_shape=jax.ShapeDtypeStruct((B, S, D), BF16),
        scratch_shapes=[pltpu.VMEM((ts + CONV_HALO, C), F32), pltpu.VMEM((ts, C), BF16)],
        compiler_params=_params("parallel", "arbitrary"),
        name="conv_branch",
    )(proj, proj, proj, dw_w, dw_b.reshape(1, C), ln_g.reshape(1, C), ln_b.reshape(1, C), w_out,
      b_out.reshape(1, D))


def _moba_kernel(slope_ref, q_ref, k_ref, v_ref, o_ref, km_ref, bias_ref, m_ref, l_ref, acc_ref, *, nb, scale):
    L = MOBA_BLOCK
    qi = pl.program_id(2)

    @pl.when(qi == 0)
    def _():
        km_ref[...] = jnp.zeros(km_ref.shape, F32)
        for n in range(nb):
            km_ref[n:n + 1, :] = jnp.mean(k_ref[0, n * L:(n + 1) * L, :].astype(F32), axis=0, keepdims=True)

    q = q_ref[0]
    slope = slope_ref[0]

    km = km_ref[...]
    km_hi = km.astype(BF16)
    km_lo = (km - km_hi.astype(F32)).astype(BF16)
    gate = _dot_nt(km_hi, q) + _dot_nt(km_lo, q)
    nrow = lax.broadcasted_iota(jnp.int32, gate.shape, 0)
    rank = jnp.zeros(gate.shape, F32)
    for m in range(nb):
        gm = gate[m:m + 1, :]
        ahead = (gm > gate) | ((gm == gate) & (nrow > m))
        rank = rank + jnp.where(ahead & (qi > m), 1.0, 0.0)
    chosen = (rank < float(MOBA_TOPK)) & (nrow < qi)
    bias_t = jnp.where(chosen, 0.0, NEG)
    pad = jnp.zeros((LANES - bias_t.shape[0], L), F32)
    bias_ref[...] = jnp.concatenate([bias_t, pad], axis=0).T

    row = lax.broadcasted_iota(jnp.int32, (L, L), 0)
    col = lax.broadcasted_iota(jnp.int32, (L, L), 1)
    dist = (col - row).astype(F32)

    k_own = k_ref[0, pl.ds(pl.multiple_of(qi * L, L), L), :]
    v_own = v_ref[0, pl.ds(pl.multiple_of(qi * L, L), L), :]
    s = _dot_nt(q, k_own) * scale + slope * dist
    s = jnp.where(col <= row, s, NEG)
    m0 = jnp.max(s, axis=-1, keepdims=True)
    p = jnp.exp(s - m0)
    m_ref[...] = m0
    l_ref[...] = jnp.sum(p, axis=-1, keepdims=True)
    acc_ref[...] = jnp.dot(p.astype(BF16), v_own, preferred_element_type=F32)

    for n in range(nb - 1):
        @pl.when(n < qi)
        def _():
            kb = k_ref[0, n * L:(n + 1) * L, :]
            vb = v_ref[0, n * L:(n + 1) * L, :]
            off = ((n - qi) * L).astype(F32)
            sb = _dot_nt(q, kb) * scale + slope * (dist + off) + bias_ref[:, n:n + 1]
            m_prev = m_ref[...]
            m_new = jnp.maximum(m_prev, jnp.max(sb, axis=-1, keepdims=True))
            alpha = jnp.exp(m_prev - m_new)
            pb = jnp.exp(sb - m_new)
            l_ref[...] = alpha * l_ref[...] + jnp.sum(pb, axis=-1, keepdims=True)
            acc_ref[...] = alpha * acc_ref[...] + jnp.dot(pb.astype(BF16), vb, preferred_element_type=F32)
            m_ref[...] = m_new

    o_ref[0] = (acc_ref[...] / l_ref[...]).astype(o_ref.dtype)


def _moba(proj, slopes, S, col_q, col_k, col_v):
    B = proj.shape[0]
    H = ATTN_HEADS
    L = MOBA_BLOCK
    nb = S // L
    hd = LANES
    assert nb * L == S and nb <= 2 * SUBLANES
    slope_rows = jnp.broadcast_to(slopes.reshape(H, 1, 1), (H, 1, L)).astype(F32)
    return pl.pallas_call(
        functools.partial(_moba_kernel, nb=nb, scale=hd ** -0.5),
        grid=(B, H, nb),
        in_specs=[pl.BlockSpec((1, 1, L), lambda b, h, i: (h, 0, 0)),
                  pl.BlockSpec((1, L, hd), lambda b, h, i: (b, i, col_q + h)),
                  pl.BlockSpec((1, S, hd), lambda b, h, i: (b, 0, col_k + h)),
                  pl.BlockSpec((1, S, hd), lambda b, h, i: (b, 0, col_v + h))],
        out_specs=pl.BlockSpec((1, L, hd), lambda b, h, i: (b, i, h)),
        out_shape=jax.ShapeDtypeStruct((B, S, H * hd), BF16),
        scratch_shapes=[pltpu.VMEM((2 * SUBLANES, hd), F32),
                        pltpu.VMEM((L, LANES), F32),
                        pltpu.VMEM((L, 1), F32), pltpu.VMEM((L, 1), F32), pltpu.VMEM((L, hd), F32)],
        compiler_params=_params("parallel", "parallel", "arbitrary"),
        name="moba_attention",
    )(slope_rows, proj, proj, proj)


def _mix_kernel(attn_ref, ga_ref, cg_ref, x_ref, wao_ref, wmo_ref, o_ref):
    ao = jnp.dot(attn_ref[...], wao_ref[...], preferred_element_type=F32)
    merged = cg_ref[...].astype(F32) + jax.nn.sigmoid(ga_ref[...].astype(F32)) * ao
    o_ref[...] = x_ref[...] + jnp.dot(merged.astype(BF16), wmo_ref[...], preferred_element_type=F32)


def _mix(attn, proj2d, cg, x, w_ao, w_mo, tm, col_ga):
    T, D = x.shape
    A = attn.shape[1]
    tok = lambda c: pl.BlockSpec((tm, c), lambda i: (i, 0))
    return pl.pallas_call(
        _mix_kernel,
        grid=(T // tm,),
        in_specs=[tok(A), pl.BlockSpec((tm, D), lambda i: (i, col_ga)), tok(D), tok(D),
                  pl.BlockSpec((A, D), lambda i: (0, 0)), pl.BlockSpec((D, D), lambda i: (0, 0))],
        out_specs=tok(D),
        out_shape=jax.ShapeDtypeStruct((T, D), F32),
        compiler_params=_params("parallel"),
        name="attn_out_merge_mix",
    )(attn, proj2d, cg, x, w_ao, w_mo)


def _xattn_kernel(h_ref, g_ref, wq_ref, kv_ref, wo_ref, o_ref, oh_ref):
    D = h_ref.shape[-1]
    hd = D // XATTN_HEADS
    h = h_ref[0]
    hn = _rms(h, g_ref[...]).astype(BF16)
    q = (jnp.dot(hn, wq_ref[...], preferred_element_type=F32) * (hd ** -0.5)).astype(BF16)
    for i in range(XATTN_HEADS):
        kh = kv_ref[0, :, i * hd:(i + 1) * hd]
        vh = kv_ref[0, :, D + i * hd:D + (i + 1) * hd]
        s = _dot_nt(q[:, i * hd:(i + 1) * hd], kh)
        p = jnp.exp(s - jnp.max(s, axis=-1, keepdims=True))
        l = jnp.sum(p, axis=-1, keepdims=True)
        oh_ref[:, i * hd:(i + 1) * hd] = (jnp.dot(p.astype(BF16), vh, preferred_element_type=F32) / l).astype(BF16)
    o_ref[0] = h + jnp.dot(oh_ref[...], wo_ref[...], preferred_element_type=F32)


def _xattn(h, g, w_q, kv, w_o, tm):
    B, S, D = h.shape
    M = kv.shape[1]
    return pl.pallas_call(
        _xattn_kernel,
        grid=(B, S // tm),
        in_specs=[pl.BlockSpec((1, tm, D), lambda b, i: (b, i, 0)),
                  pl.BlockSpec((1, D), lambda b, i: (0, 0)),
                  pl.BlockSpec((D, D), lambda b, i: (0, 0)),
                  pl.BlockSpec((1, M, 2 * D), lambda b, i: (b, 0, 0)),
                  pl.BlockSpec((D, D), lambda b, i: (0, 0))],
        out_specs=pl.BlockSpec((1, tm, D), lambda b, i: (b, i, 0)),
        out_shape=jax.ShapeDtypeStruct((B, S, D), F32),
        scratch_shapes=[pltpu.VMEM((tm, D), BF16)],
        compiler_params=_params("parallel", "parallel"),
        name="memory_cross_attention",
    )(h, g.reshape(1, D), w_q, kv, w_o)


def _sort_pairs(n):
    pairs = []
    p = 1
    while p < n:
        k = p
        while k >= 1:
            for j in range(k % p, n - k, 2 * k):
                for i in range(min(k, n - j - k)):
                    if (i + j) // (2 * p) == (i + j + k) // (2 * p):
                        pairs.append((i + j, i + j + k))
            k //= 2
        p *= 2
    return pairs


def _sort_desc(xs):
    xs = list(xs)
    for i, j in _sort_pairs(len(xs)):
        hi, lo = jnp.maximum(xs[i], xs[j]), jnp.minimum(xs[i], xs[j])
        xs[i], xs[j] = hi, lo
    return xs


def _merge_top(a, b):
    n = len(a)
    xs = [jnp.maximum(a[i], b[n - 1 - i]) for i in range(n)]
    d = n // 2
    while d >= 1:
        for i in range(n):
            if (i // d) % 2 == 0:
                hi, lo = jnp.maximum(xs[i], xs[i + d]), jnp.minimum(xs[i], xs[i + d])
                xs[i], xs[i + d] = hi, lo
        d //= 2
    return xs


def _top_sorted(xs, k):
    groups = [_sort_desc(xs[i:i + k]) for i in range(0, len(xs), k)]
    while len(groups) > 1:
        groups = [_merge_top(groups[i], groups[i + 1]) if i + 1 < len(groups) else groups[i]
                  for i in range(0, len(groups), 2)]
    return groups[0]


def _peer_prep_kernel(h_ref, g_ref, wq_ref, wkh_ref, whk_ref, xn_ref, s1_ref, s2_ref, e2_ref, tau_ref, mc_ref,
                      skh_ref):
    K = PEER_TOPK
    Hp = PEER_HEADS
    W = wkh_ref.shape[-1]
    xn = _rms(h_ref[...], g_ref[...]).astype(BF16)
    xn_ref[...] = xn
    q = jnp.dot(xn, wq_ref[...], preferred_element_type=F32).astype(BF16)

    tops = []
    for half in range(2):
        qh = q[:, half * W:(half + 1) * W]
        skh_ref[...] = _dot_nt(wkh_ref[half], qh)
        if half == 0:
            s1_ref[...] = skh_ref[...]
        tops.append(_top_sorted([skh_ref[k * Hp:(k + 1) * Hp, :] for k in range(PEER_NKEYS)], K))
    v1, v2 = tops

    cand = [v1[i] + v2[j] for i in range(K) for j in range(K) if (i + 1) * (j + 1) <= K]
    cand += [jnp.full_like(cand[0], -jnp.inf)] * (-len(cand) % K)
    top = _top_sorted(cand, K)
    z = functools.reduce(lambda a, b: a + b, [jnp.exp(t - top[0]) for t in top])
    tau_ref[...] = top[K - 1]
    m2 = v2[0]
    mc_ref[...] = m2 - (top[0] + jnp.log(z))

    s2 = _dot_nt(whk_ref[...], q[:, W:2 * W])
    for h in range(Hp):
        blk = s2[h * PEER_NKEYS:(h + 1) * PEER_NKEYS, :]
        s2_ref[h] = blk
        e2_ref[h] = jnp.exp(blk - m2[h:h + 1, :])


def _peer_prep(h, g, wq, wkh, whk, tm):
    T, D = h.shape
    Hp, NK = PEER_HEADS, PEER_NKEYS
    Wq = wq.shape[1]
    W = wkh.shape[-1]
    const2 = lambda shape: pl.BlockSpec(shape, lambda i: (0,) * len(shape))
    return pl.pallas_call(
        _peer_prep_kernel,
        grid=(T // tm,),
        in_specs=[pl.BlockSpec((tm, D), lambda i: (i, 0)), const2((1, D)), const2((D, Wq)),
                  const2((2, NK * Hp, W)), const2((Hp * NK, W))],
        out_specs=[pl.BlockSpec((tm, D), lambda i: (i, 0)),
                   pl.BlockSpec((NK * Hp, tm), lambda i: (0, i)),
                   pl.BlockSpec((Hp, NK, tm), lambda i: (0, 0, i)),
                   pl.BlockSpec((Hp, NK, tm), lambda i: (0, 0, i)),
                   pl.BlockSpec((Hp, tm), lambda i: (0, i)),
                   pl.BlockSpec((Hp, tm), lambda i: (0, i))],
        out_shape=[jax.ShapeDtypeStruct((T, D), BF16),
                   jax.ShapeDtypeStruct((NK * Hp, T), F32),
                   jax.ShapeDtypeStruct((Hp, NK, T), F32),
                   jax.ShapeDtypeStruct((Hp, NK, T), F32),
                   jax.ShapeDtypeStruct((Hp, T), F32),
                   jax.ShapeDtypeStruct((Hp, T), F32)],
        scratch_shapes=[pltpu.VMEM((NK * Hp, tm), F32)],
        compiler_params=_params("parallel"),
        name="peer_retrieval",
    )(h, g.reshape(1, D), wq, wkh, whk)


def _peer_expert_kernel(xn_ref, u_ref, vt_ref, s1_ref, s2_ref, e2_ref, tau_ref, mc_ref, h_ref, g_ref, o_ref,
                        acc_ref, at_ref, gt_ref, *, te, tm):
    j = pl.program_id(1)
    NK, Hp = PEER_NKEYS, PEER_HEADS

    @pl.when(j == 0)
    def _():
        acc_ref[...] = jnp.zeros(acc_ref.shape, F32)

    at_ref[...] = _dot_nt(u_ref[...], xn_ref[...])
    for r in range(te // NK):
        for c in range(tm // LANES):
            cs = slice(c * LANES, (c + 1) * LANES)
            w = jnp.zeros((NK, LANES), F32)
            for h in range(Hp):
                s1 = s1_ref[r * Hp + h:r * Hp + h + 1, cs]
                e1 = jnp.exp(s1 + mc_ref[h:h + 1, cs])
                chosen = (s2_ref[h, :, cs] + s1) >= tau_ref[h:h + 1, cs]
                w = w + jnp.where(chosen, e2_ref[h, :, cs], 0.0) * e1
            a = at_ref[r * NK:(r + 1) * NK, cs]
            gelu = 0.5 * a * (1.0 + lax.erf(a * math.sqrt(0.5)))
            gt_ref[r * NK:(r + 1) * NK, cs] = (gelu * w).astype(BF16)
    acc_ref[...] += jnp.dot(vt_ref[...], gt_ref[...], preferred_element_type=F32)

    @pl.when(j == pl.num_programs(1) - 1)
    def _():
        o_ref[...] = _rms(h_ref[...] + acc_ref[...].T, g_ref[...])


def _peer_experts(xn, u, vt, s1, s2, e2, tau, mc, h, g, tm, te):
    T, D = h.shape
    NE = u.shape[0]
    Hp, NK = PEER_HEADS, PEER_NKEYS
    rows = te // NK * Hp
    return pl.pallas_call(
        functools.partial(_peer_expert_kernel, te=te, tm=tm),
        grid=(T // tm, NE // te),
        in_specs=[pl.BlockSpec((tm, D), lambda i, j: (i, 0)),
                  pl.BlockSpec((te, D), lambda i, j: (j, 0)),
                  pl.BlockSpec((D, te), lambda i, j: (0, j)),
                  pl.BlockSpec((rows, tm), lambda i, j: (j, i)),
                  pl.BlockSpec((Hp, NK, tm), lambda i, j: (0, 0, i)),
                  pl.BlockSpec((Hp, NK, tm), lambda i, j: (0, 0, i)),
                  pl.BlockSpec((Hp, tm), lambda i, j: (0, i)),
                  pl.BlockSpec((Hp, tm), lambda i, j: (0, i)),
                  pl.BlockSpec((tm, D), lambda i, j: (i, 0)),
                  pl.BlockSpec((1, D), lambda i, j: (0, 0))],
        out_specs=pl.BlockSpec((tm, D), lambda i, j: (i, 0)),
        out_shape=jax.ShapeDtypeStruct((T, D), F32),
        scratch_shapes=[pltpu.VMEM((D, tm), F32), pltpu.VMEM((te, tm), F32), pltpu.VMEM((te, tm), BF16)],
        compiler_params=_params("parallel", "arbitrary"),
        name="peer_experts",
    )(xn, u, vt, s1, s2, e2, tau, mc, h, g.reshape(1, D))


def _layer(h, mem, mix_norm_g, w_in, conv_dw_w, conv_dw_b, conv_ln_g, conv_ln_b, w_conv_out, b_conv_out,
           w_attn_out, w_mix_out, xattn_norm_g, mem_norm_g, w_xq, w_xkv, w_xo, ffn_norm_g, w_peer_q,
           peer_subkeys, peer_u, peer_v, out_norm_g):
    B, S, D = h.shape
    T = B * S
    M = mem.shape[1]
    A = ATTN_HEADS * LANES
    C = conv_dw_w.shape[-1]
    assert A == D and C == D, "column-block addressing of the combined projection assumes equal widths"
    Hp, NK = PEER_HEADS, PEER_NKEYS
    half = peer_subkeys.shape[-1]

    x2 = h.reshape(T, D)
    proj = _rms_matmul(x2, mix_norm_g, w_in.astype(BF16), tm=min(1024, T), tn=D)
    proj3 = proj.reshape(B, S, -1)
    cg = _conv_branch(proj3, conv_dw_w, conv_dw_b, conv_ln_g, conv_ln_b, w_conv_out.astype(BF16), b_conv_out,
                      ts=min(512, S), col_a=3, col_b=4, col_g=5)
    slopes = 2.0 ** (-8.0 * jnp.arange(1, ATTN_HEADS + 1, dtype=F32) / ATTN_HEADS)
    attn = _moba(proj3, slopes, S, col_q=0, col_k=ATTN_HEADS, col_v=2 * ATTN_HEADS)
    h1 = _mix(attn.reshape(T, A), proj, cg.reshape(T, D), x2, w_attn_out.astype(BF16), w_mix_out.astype(BF16),
              tm=min(512, T), col_ga=6)

    kv = _rms_matmul(mem.reshape(B * M, D), mem_norm_g, w_xkv.astype(BF16), tm=min(512, B * M), tn=D)
    h2 = _xattn(h1.reshape(B, S, D), xattn_norm_g, w_xq.astype(BF16), kv.reshape(B, M, 2 * D), w_xo.astype(BF16),
                tm=min(512, S)).reshape(T, D)

    wq = w_peer_q.reshape(D, Hp, 2, half).transpose(0, 2, 1, 3).reshape(D, 2 * Hp * half).astype(BF16)
    eye = jnp.eye(Hp, dtype=F32)
    sk = peer_subkeys.astype(F32)
    wkh = jnp.einsum('hpkd,hg->pkhgd', sk, eye).reshape(2, NK * Hp, Hp * half).astype(BF16)
    whk = jnp.einsum('hkd,hg->hkgd', sk[:, 1], eye).reshape(Hp * NK, Hp * half).astype(BF16)
    xn3, s1, s2, e2, tau, mc = _peer_prep(h2, ffn_norm_g, wq, wkh, whk, tm=min(256, T))
    return _peer_experts(xn3, peer_u.astype(BF16), peer_v.T.astype(BF16), s1, s2, e2, tau, mc, h2, out_norm_g,
                         tm=min(512, T), te=512)


def kernel(x, mem, mix_norm_g, w_in, conv_dw_w, conv_dw_b, conv_ln_g, conv_ln_b, w_conv_out, b_conv_out, w_attn_out, w_mix_out, xattn_norm_g, mem_norm_g, w_xq, w_xkv, w_xo, ffn_norm_g, w_peer_q, peer_subkeys, peer_u, peer_v, final_norm_g):
    depth = w_in.shape[0]
    assert depth == 1, "the last layer's kernel applies the final norm; deeper stacks need a plain-residual variant"
    B, S, D = x.shape
    out = _layer(x, mem, mix_norm_g[0], w_in[0], conv_dw_w[0], conv_dw_b[0], conv_ln_g[0], conv_ln_b[0],
                 w_conv_out[0], b_conv_out[0], w_attn_out[0], w_mix_out[0], xattn_norm_g[0], mem_norm_g[0],
                 w_xq[0], w_xkv[0], w_xo[0], ffn_norm_g[0], w_peer_q[0], peer_subkeys[0], peer_u[0], peer_v[0],
                 final_norm_g)
    return out.reshape(B, S, D)
```

```python
import functools
import math

import jax
import jax.numpy as jnp
from jax import lax
from jax.experimental import pallas as pl
from jax.experimental.pallas import tpu as pltpu

F32 = jnp.float32
BF16 = jnp.bfloat16

EPS = 1e-6
ATTN_HEADS = 8
MOBA_BLOCK = 256
MOBA_TOPK = 3
CONV_WIDTH = 31
XATTN_HEADS = 4
PEER_HEADS = 8
PEER_NKEYS = 128
PEER_TOPK = 16
LOG2E = math.log2(math.e)

LANES = 128
SUBLANES = 8
MXU_WIDTH = 256
GATE_ROWS = 32
VMEM_LIMIT = 56 * 1024 * 1024
CONV_HALO = 32


def _params(*sem):
    return pltpu.CompilerParams(dimension_semantics=sem, vmem_limit_bytes=VMEM_LIMIT)


def _dot_nt(a, b):
    return lax.dot_general(a, b, (((1,), (1,)), ((), ())), preferred_element_type=F32)


def _rms(x, g):
    return x * lax.rsqrt(jnp.mean(x * x, axis=-1, keepdims=True) + EPS) * g


def _rms_matmul_kernel(x_ref, g_ref, w_ref, o_ref, xn_ref):
    @pl.when(pl.program_id(1) == 0)
    def _():
        xn_ref[...] = _rms(x_ref[...], g_ref[...]).astype(BF16)

    o_ref[...] = jnp.dot(xn_ref[...], w_ref[...], preferred_element_type=F32).astype(o_ref.dtype)


def _rms_matmul(x, g, w, tm, tn):
    T, D = x.shape
    N = w.shape[1]
    return pl.pallas_call(
        _rms_matmul_kernel,
        grid=(T // tm, N // tn),
        in_specs=[pl.BlockSpec((tm, D), lambda i, j: (i, 0)),
                  pl.BlockSpec((1, D), lambda i, j: (0, 0)),
                  pl.BlockSpec((D, tn), lambda i, j: (0, j))],
        out_specs=pl.BlockSpec((tm, tn), lambda i, j: (i, j)),
        out_shape=jax.ShapeDtypeStruct((T, N), BF16),
        scratch_shapes=[pltpu.VMEM((tm, D), BF16)],
        compiler_params=_params("parallel", "arbitrary"),
        name="rms_in_proj",
    )(x, g.reshape(1, D), w)


def _conv_kernel(a_ref, b_ref, gc_ref, dww_ref, dwb_ref, lng_ref, lnb_ref, w_ref, bo_ref, o_ref,
                 uext_ref, act_ref, *, ts, rc):
    C = a_ref.shape[-1]
    half = C // 2

    @pl.when(pl.program_id(1) == 0)
    def _():
        uext_ref[0:CONV_HALO, :] = jnp.zeros((CONV_HALO, C), F32)

    uext_ref[CONV_HALO:CONV_HALO + ts, :] = a_ref[0].astype(F32) * jax.nn.sigmoid(b_ref[0].astype(F32))

    first = CONV_HALO - (CONV_WIDTH - 1)

    def row_chunk(r, carry):
        r0 = pl.multiple_of(r * rc, rc)
        parts = []
        for c0 in (0, half):
            win = uext_ref[pl.ds(r0, rc + CONV_HALO), c0:c0 + half]
            acc = jnp.broadcast_to(dwb_ref[:, c0:c0 + half], (rc, half))
            for w in range(CONV_WIDTH):
                acc = acc + dww_ref[w:w + 1, c0:c0 + half] * win[first + w:first + w + rc, :]
            parts.append(acc)
        y = jnp.concatenate(parts, axis=1)
        mu = jnp.mean(y, axis=-1, keepdims=True)
        yc = y - mu
        var = jnp.mean(yc * yc, axis=-1, keepdims=True)
        z = yc * lax.rsqrt(var + EPS) * lng_ref[...] + lnb_ref[...]
        act_ref[pl.ds(r0, rc), :] = (z * jax.nn.sigmoid(z)).astype(BF16)
        return carry

    lax.fori_loop(0, ts // rc, row_chunk, 0)

    uext_ref[0:CONV_HALO, :] = uext_ref[ts:ts + CONV_HALO, :]
    co = jnp.dot(act_ref[...], w_ref[...], preferred_element_type=F32) + bo_ref[...]
    o_ref[0] = (jax.nn.sigmoid(gc_ref[0].astype(F32)) * co).astype(o_ref.dtype)


def _conv_branch(proj, dw_w, dw_b, ln_g, ln_b, w_out, b_out, ts, col_a, col_b, col_g):
    B, S, _ = proj.shape
    C = w_out.shape[0]
    D = w_out.shape[1]
    vec = lambda c: pl.BlockSpec((1, c), lambda b, s: (0, 0))
    return pl.pallas_call(
        functools.partial(_conv_kernel, ts=ts, rc=16),
        grid=(B, S // ts),
        in_specs=[pl.BlockSpec((1, ts, C), lambda b, s: (b, s, col_a)),
                  pl.BlockSpec((1, ts, C), lambda b, s: (b, s, col_b)),
                  pl.BlockSpec((1, ts, D), lambda b, s: (b, s, col_g)),
                  pl.BlockSpec((CONV_WIDTH, C), lambda b, s: (0, 0)),
                  vec(C), vec(C), vec(C),
                  pl.BlockSpec((C, D), lambda b, s: (0, 0)),
                  vec(D)],
        out_specs=pl.BlockSpec((1, ts, D), lambda b, s: (b, s, 0)),
        out_shape=jax.ShapeDtypeStruct((B, S, D), BF16),
        scratch_shapes=[pltpu.VMEM((ts + CONV_HALO, C), F32), pltpu.VMEM((ts, C), BF16)],
        compiler_params=_params("parallel", "arbitrary"),
        name="conv_branch",
    )(proj, proj, proj, dw_w, dw_b.reshape(1, C), ln_g.reshape(1, C), ln_b.reshape(1, C), w_out,
      b_out.reshape(1, D))


MOBA_GROUP = 4
MASK_COLS = 2 * SUBLANES
BIAS_COLS = 3
MASKED = -1e9


def _moba_kernel(q_ref, k_ref, v_ref, ka_ref, o_ref, km_ref, qa_ref, m_ref, l_ref, acc_ref, *, nb, scale):
    L = MOBA_BLOCK
    G = MOBA_GROUP
    hd = LANES
    qi = pl.program_id(2)

    @pl.when(qi == 0)
    def _():
        km_ref[...] = jnp.zeros(km_ref.shape, F32)
        for g in range(G):
            for n in range(nb):
                kb = k_ref[0, n * L:(n + 1) * L, g * hd:(g + 1) * hd].astype(F32)
                km_ref[g, n:n + 1, :] = jnp.mean(kb, axis=0, keepdims=True)

    nrow = lax.broadcasted_iota(jnp.int32, (MASK_COLS, L), 0)
    lane = lax.broadcasted_iota(jnp.int32, (L, LANES), 1)
    for g in range(G):
        q = q_ref[0, :, g * hd:(g + 1) * hd]
        km = km_ref[g]
        km_hi = km.astype(BF16)
        km_lo = (km - km_hi.astype(F32)).astype(BF16)
        gate = _dot_nt(km_hi, q) + _dot_nt(km_lo, q)
        rank = jnp.zeros(gate.shape, F32)
        for m in range(nb - 1):
            gm = gate[m:m + 1, :]
            ahead = (gm > gate) | ((gm == gate) & (nrow > m))
            rank = rank + jnp.where(ahead, 1.0, 0.0) * (qi > m).astype(F32)
        chosen = ((rank < float(MOBA_TOPK)) & (nrow < qi)) | (nrow == qi)
        bias_t = jnp.where(chosen, 0.0, MASKED)
        bias = jnp.concatenate([bias_t, jnp.zeros((LANES - MASK_COLS, L), F32)], axis=0).T
        aug = jnp.where(lane < MASK_COLS, bias, jnp.where(lane < MASK_COLS + BIAS_COLS, 1.0, 0.0))
        qa_ref[g] = jnp.concatenate([(q.astype(F32) * scale).astype(BF16), aug.astype(BF16)], axis=1)

    def scores(g, rows):
        kb = jnp.concatenate([k_ref[0, rows, g * hd:(g + 1) * hd], ka_ref[g, rows, :]], axis=1)
        return _dot_nt(qa_ref[g], kb)

    own = pl.ds(pl.multiple_of(qi * L, L), L)
    row = lax.broadcasted_iota(jnp.int32, (L, L), 0)
    col = lax.broadcasted_iota(jnp.int32, (L, L), 1)
    for g in range(G):
        s = jnp.where(col <= row, scores(g, own), MASKED)
        m0 = jnp.max(s, axis=1, keepdims=True)
        p = jnp.exp(s - m0)
        m_ref[g] = jnp.broadcast_to(m0, (L, hd))
        l_ref[g] = jnp.broadcast_to(jnp.sum(p, axis=1, keepdims=True), (L, hd))
        acc_ref[g] = jnp.dot(p.astype(BF16), v_ref[0, own, g * hd:(g + 1) * hd], preferred_element_type=F32)

    def past_block(n, carry):
        rows = pl.ds(pl.multiple_of(n * L, L), L)
        for g in range(G):
            s = scores(g, rows)
            m_prev = m_ref[g]
            m_new = jnp.maximum(m_prev, jnp.max(s, axis=1, keepdims=True))
            alpha = jnp.exp(m_prev - m_new)
            p0 = jnp.exp(s[:, :hd] - m_new)
            p1 = jnp.exp(s[:, hd:] - m_new)
            l_ref[g] = alpha * l_ref[g] + jnp.sum(p0 + p1, axis=1, keepdims=True)
            pv = jnp.dot(jnp.concatenate([p0, p1], axis=1).astype(BF16), v_ref[0, rows, g * hd:(g + 1) * hd],
                         preferred_element_type=F32)
            acc_ref[g] = alpha * acc_ref[g] + pv
            m_ref[g] = m_new
        return carry

    lax.fori_loop(0, qi, past_block, 0)

    for g in range(G):
        o_ref[0, :, g * hd:(g + 1) * hd] = (acc_ref[g] / l_ref[g]).astype(o_ref.dtype)


def _bf16_part(x):
    bits = lax.bitcast_convert_type(x, jnp.uint32) & jnp.uint32(0xFFFF0000)
    return lax.bitcast_convert_type(bits, F32)


def _moba(proj, slopes, S, col_q, col_k, col_v):
    B = proj.shape[0]
    H, G, L, hd = ATTN_HEADS, MOBA_GROUP, MOBA_BLOCK, LANES
    nb = S // L
    assert nb * L == S and nb <= MASK_COLS and H % G == 0 and col_q % G == col_k % G == col_v % G == 0
    pos = jnp.arange(S, dtype=F32)
    kbias = slopes.astype(F32)[:, None] * pos[None, :]
    hi = _bf16_part(kbias)
    mid = _bf16_part(kbias - hi)
    lo = _bf16_part(kbias - hi - mid)
    onehot = (jnp.arange(S)[:, None] // L == jnp.arange(MASK_COLS)[None, :]).astype(BF16)
    parts = [p.astype(BF16)[..., None] for p in (hi, mid, lo)]
    ka = jnp.concatenate([jnp.broadcast_to(onehot[None], (H, S, MASK_COLS))] + parts
                         + [jnp.zeros((H, S, LANES - MASK_COLS - BIAS_COLS), BF16)], axis=-1)
    return pl.pallas_call(
        functools.partial(_moba_kernel, nb=nb, scale=hd ** -0.5),
        grid=(B, H // G, nb),
        in_specs=[pl.BlockSpec((1, L, G * hd), lambda b, h, i: (b, i, col_q // G + h)),
                  pl.BlockSpec((1, S, G * hd), lambda b, h, i: (b, 0, col_k // G + h)),
                  pl.BlockSpec((1, S, G * hd), lambda b, h, i: (b, 0, col_v // G + h)),
                  pl.BlockSpec((G, S, LANES), lambda b, h, i: (h, 0, 0))],
        out_specs=pl.BlockSpec((1, L, G * hd), lambda b, h, i: (b, i, h)),
        out_shape=jax.ShapeDtypeStruct((B, S, H * hd), BF16),
        scratch_shapes=[pltpu.VMEM((G, MASK_COLS, hd), F32),
                        pltpu.VMEM((G, L, 2 * hd), BF16),
                        pltpu.VMEM((G, L, hd), F32), pltpu.VMEM((G, L, hd), F32), pltpu.VMEM((G, L, hd), F32)],
        compiler_params=_params("parallel", "parallel", "arbitrary"),
        name="moba_attention",
    )(proj, proj, proj, ka)


def _mix_kernel(attn_ref, ga_ref, cg_ref, x_ref, wao_ref, wmo_ref, o_ref):
    ao = jnp.dot(attn_ref[...], wao_ref[...], preferred_element_type=F32)
    merged = cg_ref[...].astype(F32) + jax.nn.sigmoid(ga_ref[...].astype(F32)) * ao
    o_ref[...] = x_ref[...] + jnp.dot(merged.astype(BF16), wmo_ref[...], preferred_element_type=F32)


def _mix(attn, proj2d, cg, x, w_ao, w_mo, tm, col_ga):
    T, D = x.shape
    A = attn.shape[1]
    tok = lambda c: pl.BlockSpec((tm, c), lambda i: (i, 0))
    return pl.pallas_call(
        _mix_kernel,
        grid=(T // tm,),
        in_specs=[tok(A), pl.BlockSpec((tm, D), lambda i: (i, col_ga)), tok(D), tok(D),
                  pl.BlockSpec((A, D), lambda i: (0, 0)), pl.BlockSpec((D, D), lambda i: (0, 0))],
        out_specs=tok(D),
        out_shape=jax.ShapeDtypeStruct((T, D), F32),
        compiler_params=_params("parallel"),
        name="attn_out_merge_mix",
    )(attn, proj2d, cg, x, w_ao, w_mo)


def _xattn_kernel(h_ref, g_ref, wq_ref, kv_ref, wo_ref, o_ref, oh_ref):
    D = h_ref.shape[-1]
    hd = D // XATTN_HEADS
    h = h_ref[0]
    hn = _rms(h, g_ref[...]).astype(BF16)
    q = (jnp.dot(hn, wq_ref[...], preferred_element_type=F32) * (hd ** -0.5)).astype(BF16)
    for i in range(XATTN_HEADS):
        kh = kv_ref[0, :, i * hd:(i + 1) * hd]
        vh = kv_ref[0, :, D + i * hd:D + (i + 1) * hd]
        s = _dot_nt(q[:, i * hd:(i + 1) * hd], kh)
        p = jnp.exp(s - jnp.max(s, axis=-1, keepdims=True))
        l = jnp.sum(p, axis=-1, keepdims=True)
        oh_ref[:, i * hd:(i + 1) * hd] = (jnp.dot(p.astype(BF16), vh, preferred_element_type=F32) / l).astype(BF16)
    o_ref[0] = h + jnp.dot(oh_ref[...], wo_ref[...], preferred_element_type=F32)


def _xattn(h, g, w_q, kv, w_o, tm):
    B, S, D = h.shape
    M = kv.shape[1]
    return pl.pallas_call(
        _xattn_kernel,
        grid=(B, S // tm),
        in_specs=[pl.BlockSpec((1, tm, D), lambda b, i: (b, i, 0)),
                  pl.BlockSpec((1, D), lambda b, i: (0, 0)),
                  pl.BlockSpec((D, D), lambda b, i: (0, 0)),
                  pl.BlockSpec((1, M, 2 * D), lambda b, i: (b, 0, 0)),
                  pl.BlockSpec((D, D), lambda b, i: (0, 0))],
        out_specs=pl.BlockSpec((1, tm, D), lambda b, i: (b, i, 0)),
        out_shape=jax.ShapeDtypeStruct((B, S, D), F32),
        scratch_shapes=[pltpu.VMEM((tm, D), BF16)],
        compiler_params=_params("parallel", "parallel"),
        name="memory_cross_attention",
    )(h, g.reshape(1, D), w_q, kv, w_o)


def _sort_pairs(n):
    pairs = []
    p = 1
    while p < n:
        k = p
        while k >= 1:
            for j in range(k % p, n - k, 2 * k):
                for i in range(min(k, n - j - k)):
                    if (i + j) // (2 * p) == (i + j + k) // (2 * p):
                        pairs.append((i + j, i + j + k))
            k //= 2
        p *= 2
    return pairs


def _sort_desc(xs):
    xs = list(xs)
    for i, j in _sort_pairs(len(xs)):
        hi, lo = jnp.maximum(xs[i], xs[j]), jnp.minimum(xs[i], xs[j])
        xs[i], xs[j] = hi, lo
    return xs


def _merge_top(a, b):
    n = len(a)
    xs = [jnp.maximum(a[i], b[n - 1 - i]) for i in range(n)]
    d = n // 2
    while d >= 1:
        for i in range(n):
            if (i // d) % 2 == 0:
                hi, lo = jnp.maximum(xs[i], xs[i + d]), jnp.minimum(xs[i], xs[i + d])
                xs[i], xs[i + d] = hi, lo
        d //= 2
    return xs


def _top_sorted(xs, k):
    groups = [_sort_desc(xs[i:i + k]) for i in range(0, len(xs), k)]
    while len(groups) > 1:
        groups = [_merge_top(groups[i], groups[i + 1]) if i + 1 < len(groups) else groups[i]
                  for i in range(0, len(groups), 2)]
    return groups[0]


def _peer_prep_kernel(h_ref, g_ref, wq_ref, wkh_ref, whk_ref, xnt_ref, s1_ref, s2_ref, tau_ref, skh_ref):
    K = PEER_TOPK
    Hp = PEER_HEADS
    NK = PEER_NKEYS
    W = wkh_ref.shape[-1]
    xn = _rms(h_ref[...], g_ref[...]).astype(BF16)
    xnt_ref[...] = _rms(h_ref[...], g_ref[...]).T.astype(BF16)
    q = jnp.dot(xn, wq_ref[...], preferred_element_type=F32).astype(BF16)

    tops = []
    for half in range(2):
        skh_ref[half] = _dot_nt(wkh_ref[half], q[:, half * W:(half + 1) * W]) * LOG2E
        tops.append(_top_sorted([skh_ref[half, k * Hp:(k + 1) * Hp, :] for k in range(NK)], K))
    v1, v2 = tops

    pairs = [(i, j) for i in range(K) for j in range(K) if (i + 1) * (j + 1) <= K]

    def top_sums(first):
        cand = [v2[j] + first[i] for i, j in pairs]
        cand += [jnp.full_like(cand[0], -jnp.inf)] * (-len(cand) % K)
        return _top_sorted(cand, K)

    mx = v1[0] + v2[0]
    c1 = [v - mx for v in v1]
    z = functools.reduce(lambda a, b: a + b, [jnp.exp2(t) for t in top_sums(c1)])
    lz = jnp.log(z) * LOG2E + 1.0
    tau_ref[...] = top_sums([c - lz for c in c1])[K - 1]
    for k in range(NK):
        s1_ref[k * Hp:(k + 1) * Hp, :] = (skh_ref[0, k * Hp:(k + 1) * Hp, :] - mx) - lz

    s2 = _dot_nt(whk_ref[...], q[:, W:2 * W]) * LOG2E
    for h in range(Hp):
        s2_ref[h] = s2[h * NK:(h + 1) * NK, :]


def _peer_prep(h, g, wq, wkh, whk, tm):
    T, D = h.shape
    Hp, NK = PEER_HEADS, PEER_NKEYS
    Wq = wq.shape[1]
    W = wkh.shape[-1]
    const2 = lambda shape: pl.BlockSpec(shape, lambda i: (0,) * len(shape))
    return pl.pallas_call(
        _peer_prep_kernel,
        grid=(T // tm,),
        in_specs=[pl.BlockSpec((tm, D), lambda i: (i, 0)), const2((1, D)), const2((D, Wq)),
                  const2((2, NK * Hp, W)), const2((Hp * NK, W))],
        out_specs=[pl.BlockSpec((D, tm), lambda i: (0, i)),
                   pl.BlockSpec((NK * Hp, tm), lambda i: (0, i)),
                   pl.BlockSpec((Hp, NK, tm), lambda i: (0, 0, i)),
                   pl.BlockSpec((Hp, tm), lambda i: (0, i))],
        out_shape=[jax.ShapeDtypeStruct((D, T), BF16),
                   jax.ShapeDtypeStruct((NK * Hp, T), F32),
                   jax.ShapeDtypeStruct((Hp, NK, T), F32),
                   jax.ShapeDtypeStruct((Hp, T), F32)],
        scratch_shapes=[pltpu.VMEM((2, NK * Hp, tm), F32)],
        compiler_params=_params("parallel"),
        name="peer_retrieval",
    )(h, g.reshape(1, D), wq, wkh, whk)


def _peer_expert_kernel(xnt_ref, u_ref, vt_ref, s1_ref, s2_ref, tau_ref, h_ref, g_ref, o_ref,
                        acc_ref, at0_ref, at1_ref, gt0_ref, gt1_ref, *, te, tm):
    j = pl.program_id(1)
    NK, Hp = PEER_NKEYS, PEER_HEADS

    @pl.when(j == 0)
    def _():
        acc_ref[...] = jnp.zeros(acc_ref.shape, F32)
        for ref in (at0_ref, at1_ref, gt0_ref, gt1_ref):
            ref[...] = jnp.zeros(ref.shape, ref.dtype)

    def gate_block(at_prev, gt_prev, r, c, k0):
        cs = slice(c * LANES, (c + 1) * LANES)
        ks = slice(k0, k0 + GATE_ROWS)
        w = jnp.zeros((GATE_ROWS, LANES), F32)
        for h in range(Hp):
            total = s2_ref[h, ks, cs] + s1_ref[r * Hp + h:r * Hp + h + 1, cs]
            w = w + jnp.where(total >= tau_ref[h:h + 1, cs], jnp.exp2(total), 0.0)
        rows = slice(r * NK + k0, r * NK + k0 + GATE_ROWS)
        a = at_prev[rows, cs]
        gt_prev[rows, cs] = (a * (1.0 + lax.erf(a * math.sqrt(0.5))) * w).astype(BF16)

    def step(at_cur, at_prev, gt_cur, gt_prev):
        W = MXU_WIDTH
        D = u_ref.shape[1]
        pieces = []
        for n in range(tm // W):
            ts = slice(n * W, (n + 1) * W)
            pieces += [("scores", ts, slice(k * D // 2, (k + 1) * D // 2), k) for k in range(2)]
        for n in range(tm // W):
            ts = slice(n * W, (n + 1) * W)
            pieces += [("values", ts, slice(k * W, (k + 1) * W), k) for k in range(te // W)]
        blocks = [(r, c, k0) for r in range(te // NK) for c in range(tm // LANES) for k0 in range(0, NK, GATE_ROWS)]
        per = len(blocks) // len(pieces)
        for i, (kind, ts, kk, k) in enumerate(pieces):
            if kind == "scores":
                part = jnp.dot(u_ref[:, kk], xnt_ref[kk, ts], preferred_element_type=F32)
                if k == 0:
                    at_cur[:, ts] = part
                else:
                    at_cur[:, ts] += part
            else:
                acc_ref[:, ts] += jnp.dot(vt_ref[:, kk], gt_cur[kk, ts], preferred_element_type=F32)
            for blk in blocks[i * per:(i + 1) * per]:
                gate_block(at_prev, gt_prev, *blk)

    pl.when(j % 2 == 0)(lambda: step(at0_ref, at1_ref, gt0_ref, gt1_ref))
    pl.when(j % 2 == 1)(lambda: step(at1_ref, at0_ref, gt1_ref, gt0_ref))

    @pl.when(j == pl.num_programs(1) - 1)
    def _():
        o_ref[...] = _rms(h_ref[...] + acc_ref[...].T, g_ref[...])


def _peer_experts(xnt, u, vt, s1, s2, tau, h, g, tm, te):
    T, D = h.shape
    NE = u.shape[0]
    Hp, NK = PEER_HEADS, PEER_NKEYS
    rows = te // NK * Hp
    nj = NE // te
    tile = lambda j, lag: jnp.clip(j - lag, 0, nj - 1)
    return pl.pallas_call(
        functools.partial(_peer_expert_kernel, te=te, tm=tm),
        grid=(T // tm, nj + 2),
        in_specs=[pl.BlockSpec((D, tm), lambda i, j: (0, i)),
                  pl.BlockSpec((te, D), lambda i, j: (tile(j, 0), 0)),
                  pl.BlockSpec((D, te), lambda i, j: (0, tile(j, 2))),
                  pl.BlockSpec((rows, tm), lambda i, j: (tile(j, 1), i)),
                  pl.BlockSpec((Hp, NK, tm), lambda i, j: (0, 0, i)),
                  pl.BlockSpec((Hp, tm), lambda i, j: (0, i)),
                  pl.BlockSpec((tm, D), lambda i, j: (i, 0)),
                  pl.BlockSpec((1, D), lambda i, j: (0, 0))],
        out_specs=pl.BlockSpec((tm, D), lambda i, j: (i, 0)),
        out_shape=jax.ShapeDtypeStruct((T, D), F32),
        scratch_shapes=[pltpu.VMEM((D, tm), F32), pltpu.VMEM((te, tm), F32), pltpu.VMEM((te, tm), F32),
                        pltpu.VMEM((te, tm), BF16), pltpu.VMEM((te, tm), BF16)],
        compiler_params=_params("parallel", "arbitrary"),
        name="peer_experts",
    )(xnt, u, vt, s1, s2, tau, h, g.reshape(1, D))


def _layer(h, mem, mix_norm_g, w_in, conv_dw_w, conv_dw_b, conv_ln_g, conv_ln_b, w_conv_out, b_conv_out,
           w_attn_out, w_mix_out, xattn_norm_g, mem_norm_g, w_xq, w_xkv, w_xo, ffn_norm_g, w_peer_q,
           peer_subkeys, peer_u, peer_v, out_norm_g):
    B, S, D = h.shape
    T = B * S
    M = mem.shape[1]
    A = ATTN_HEADS * LANES
    C = conv_dw_w.shape[-1]
    assert A == D and C == D, "column-block addressing of the combined projection assumes equal widths"
    Hp, NK = PEER_HEADS, PEER_NKEYS
    half = peer_subkeys.shape[-1]

    x2 = h.reshape(T, D)
    proj = _rms_matmul(x2, mix_norm_g, w_in.astype(BF16), tm=min(1024, T), tn=D)
    proj3 = proj.reshape(B, S, -1)
    cg = _conv_branch(proj3, conv_dw_w, conv_dw_b, conv_ln_g, conv_ln_b, w_conv_out.astype(BF16), b_conv_out,
                      ts=min(512, S), col_a=3, col_b=4, col_g=5)
    slopes = 2.0 ** (-8.0 * jnp.arange(1, ATTN_HEADS + 1, dtype=F32) / ATTN_HEADS)
    attn = _moba(proj3, slopes, S, col_q=0, col_k=ATTN_HEADS, col_v=2 * ATTN_HEADS)
    h1 = _mix(attn.reshape(T, A), proj, cg.reshape(T, D), x2, w_attn_out.astype(BF16), w_mix_out.astype(BF16),
              tm=min(512, T), col_ga=6)

    kv = _rms_matmul(mem.reshape(B * M, D), mem_norm_g, w_xkv.astype(BF16), tm=min(512, B * M), tn=D)
    h2 = _xattn(h1.reshape(B, S, D), xattn_norm_g, w_xq.astype(BF16), kv.reshape(B, M, 2 * D), w_xo.astype(BF16),
                tm=min(512, S)).reshape(T, D)

    wq = w_peer_q.reshape(D, Hp, 2, half).transpose(0, 2, 1, 3).reshape(D, 2 * Hp * half).astype(BF16)
    eye = jnp.eye(Hp, dtype=F32)
    sk = peer_subkeys.astype(F32)
    wkh = jnp.einsum('hpkd,hg->pkhgd', sk, eye).reshape(2, NK * Hp, Hp * half).astype(BF16)
    whk = jnp.einsum('hkd,hg->hkgd', sk[:, 1], eye).reshape(Hp * NK, Hp * half).astype(BF16)
    xnt, s1, s2, tau = _peer_prep(h2, ffn_norm_g, wq, wkh, whk, tm=min(256, T))
    return _peer_experts(xnt, peer_u.astype(BF16), peer_v.T.astype(BF16), s1, s2, tau, h2, out_norm_g,
                         tm=min(512, T), te=512)


def kernel(x, mem, mix_norm_g, w_in, conv_dw_w, conv_dw_b, conv_ln_g, conv_ln_b, w_conv_out, b_conv_out, w_attn_out, w_mix_out, xattn_norm_g, mem_norm_g, w_xq, w_xkv, w_xo, ffn_norm_g, w_peer_q, peer_subkeys, peer_u, peer_v, final_norm_g):
    depth = w_in.shape[0]
    assert depth == 1, "the last layer's kernel applies the final norm; deeper stacks need a plain-residual variant"
    B, S, D = x.shape
    out = _layer(x, mem, mix_norm_g[0], w_in[0], conv_dw_w[0], conv_dw_b[0], conv_ln_g[0], conv_ln_b[0],
                 w_conv_out[0], b_conv_out[0], w_attn_out[0], w_mix_out[0], xattn_norm_g[0], mem_norm_g[0],
                 w_xq[0], w_xkv[0], w_xo[0], ffn_norm_g[0], w_peer_q[0], peer_subkeys[0], peer_u[0], peer_v[0],
                 final_norm_g)
    return out.reshape(B, S, D)
```

```python
import functools
import math

import jax
import jax.numpy as jnp
from jax import lax
from jax.experimental import pallas as pl
from jax.experimental.pallas import tpu as pltpu

F32 = jnp.float32
BF16 = jnp.bfloat16

EPS = 1e-6
ATTN_HEADS = 8
MOBA_BLOCK = 256
MOBA_TOPK = 3
CONV_WIDTH = 31
XATTN_HEADS = 4
PEER_HEADS = 8
PEER_NKEYS = 128
PEER_TOPK = 16
LOG2E = math.log2(math.e)

LANES = 128
SUBLANES = 8
MXU_WIDTH = 256
GATE_ROWS = 32
VMEM_LIMIT = 56 * 1024 * 1024
CONV_HALO = 32


def _params(*sem):
    return pltpu.CompilerParams(dimension_semantics=sem, vmem_limit_bytes=VMEM_LIMIT)


def _dot_nt(a, b):
    return lax.dot_general(a, b, (((1,), (1,)), ((), ())), preferred_element_type=F32)


def _rms(x, g):
    return x * lax.rsqrt(jnp.mean(x * x, axis=-1, keepdims=True) + EPS) * g


def _rms_matmul_kernel(x_ref, g_ref, w_ref, o_ref, xn_ref):
    @pl.when(pl.program_id(1) == 0)
    def _():
        xn_ref[...] = _rms(x_ref[...], g_ref[...]).astype(BF16)

    o_ref[...] = jnp.dot(xn_ref[...], w_ref[...], preferred_element_type=F32).astype(o_ref.dtype)


def _rms_matmul(x, g, w, tm, tn):
    T, D = x.shape
    N = w.shape[1]
    return pl.pallas_call(
        _rms_matmul_kernel,
        grid=(T // tm, N // tn),
        in_specs=[pl.BlockSpec((tm, D), lambda i, j: (i, 0)),
                  pl.BlockSpec((1, D), lambda i, j: (0, 0)),
                  pl.BlockSpec((D, tn), lambda i, j: (0, j))],
        out_specs=pl.BlockSpec((tm, tn), lambda i, j: (i, j)),
        out_shape=jax.ShapeDtypeStruct((T, N), BF16),
        scratch_shapes=[pltpu.VMEM((tm, D), BF16)],
        compiler_params=_params("parallel", "arbitrary"),
        name="rms_in_proj",
    )(x, g.reshape(1, D), w)


def _conv_kernel(a_ref, b_ref, gc_ref, dww_ref, dwb_ref, lng_ref, lnb_ref, w_ref, bo_ref, o_ref,
                 ph_ref, act_ref, *, ts, rc):
    C = a_ref.shape[-1]
    half = C // 2
    ext = CONV_HALO + ts

    @pl.when(pl.program_id(1) == 0)
    def _():
        ph_ref[0, 0:CONV_HALO, :] = jnp.zeros((CONV_HALO, C), F32)
        ph_ref[0, ext:ext + SUBLANES, :] = jnp.zeros((SUBLANES, C), F32)

    ph_ref[0, CONV_HALO:ext, :] = a_ref[0].astype(F32) * jax.nn.sigmoid(b_ref[0].astype(F32))

    def phase_chunk(i, carry):
        r0 = pl.multiple_of(i * CONV_HALO, CONV_HALO)
        for c0 in (0, half):
            win = ph_ref[0, pl.ds(r0, CONV_HALO + SUBLANES), c0:c0 + half]
            for j in range(1, SUBLANES):
                ph_ref[j, pl.ds(r0, CONV_HALO), c0:c0 + half] = win[j:j + CONV_HALO, :]
        return carry

    lax.fori_loop(0, ext // CONV_HALO, phase_chunk, 0)

    first = CONV_HALO - (CONV_WIDTH - 1)

    def row_chunk(r, carry):
        r0 = pl.multiple_of(r * rc, rc)
        parts = []
        for c0 in (0, half):
            accs = [dwb_ref[:, c0:c0 + half]] * (rc // SUBLANES)
            for w in range(CONV_WIDTH):
                k, j = divmod(first + w, SUBLANES)
                wgt = dww_ref[w, :, c0:c0 + half]
                for g in range(rc // SUBLANES):
                    rows = pl.ds(pl.multiple_of(r0 + (k + g) * SUBLANES, SUBLANES), SUBLANES)
                    accs[g] = accs[g] + wgt * ph_ref[j, rows, c0:c0 + half]
            parts.append(jnp.concatenate(accs, axis=0))
        y = jnp.concatenate(parts, axis=1)
        mu = jnp.mean(y, axis=-1, keepdims=True)
        yc = y - mu
        var = jnp.mean(yc * yc, axis=-1, keepdims=True)
        z = yc * lax.rsqrt(var + EPS) * lng_ref[...] + lnb_ref[...]
        act_ref[pl.ds(r0, rc), :] = (z * jax.nn.sigmoid(z)).astype(BF16)
        return carry

    lax.fori_loop(0, ts // rc, row_chunk, 0)

    ph_ref[0, 0:CONV_HALO, :] = ph_ref[0, ts:ts + CONV_HALO, :]
    co = jnp.dot(act_ref[...], w_ref[...], preferred_element_type=F32) + bo_ref[...]
    o_ref[0] = (jax.nn.sigmoid(gc_ref[0].astype(F32)) * co).astype(o_ref.dtype)


def _conv_branch(proj, dw_w, dw_b, ln_g, ln_b, w_out, b_out, ts, col_a, col_b, col_g):
    B, S, _ = proj.shape
    C = w_out.shape[0]
    D = w_out.shape[1]
    vec = lambda c: pl.BlockSpec((1, c), lambda b, s: (0, 0))
    return pl.pallas_call(
        functools.partial(_conv_kernel, ts=ts, rc=32),
        grid=(B, S // ts),
        in_specs=[pl.BlockSpec((1, ts, C), lambda b, s: (b, s, col_a)),
                  pl.BlockSpec((1, ts, C), lambda b, s: (b, s, col_b)),
                  pl.BlockSpec((1, ts, D), lambda b, s: (b, s, col_g)),
                  pl.BlockSpec((CONV_WIDTH, SUBLANES, C), lambda b, s: (0, 0, 0)),
                  pl.BlockSpec((SUBLANES, C), lambda b, s: (0, 0)),
                  vec(C), vec(C),
                  pl.BlockSpec((C, D), lambda b, s: (0, 0)),
                  vec(D)],
        out_specs=pl.BlockSpec((1, ts, D), lambda b, s: (b, s, 0)),
        out_shape=jax.ShapeDtypeStruct((B, S, D), BF16),
        scratch_shapes=[pltpu.VMEM((SUBLANES, CONV_HALO + ts + SUBLANES, C), F32), pltpu.VMEM((ts, C), BF16)],
        compiler_params=_params("parallel", "arbitrary"),
        name="conv_branch",
    )(proj, proj, proj,
      jnp.broadcast_to(dw_w[:, None, :], (CONV_WIDTH, SUBLANES, C)),
      jnp.broadcast_to(dw_b[None, :], (SUBLANES, C)),
      ln_g.reshape(1, C), ln_b.reshape(1, C), w_out, b_out.reshape(1, D))


MOBA_GROUP = 4
MASK_COLS = 2 * SUBLANES
BIAS_COLS = 3
MASKED = -1e9


def _moba_kernel(q_ref, k_ref, v_ref, ka_ref, o_ref, km_ref, qa_ref, m_ref, l_ref, acc_ref, *, nb, scale):
    L = MOBA_BLOCK
    G = MOBA_GROUP
    hd = LANES
    qi = pl.program_id(2)

    @pl.when(qi == 0)
    def _():
        km_ref[...] = jnp.zeros(km_ref.shape, F32)
        for g in range(G):
            for n in range(nb):
                kb = k_ref[0, n * L:(n + 1) * L, g * hd:(g + 1) * hd].astype(F32)
                km_ref[g, n:n + 1, :] = jnp.mean(kb, axis=0, keepdims=True)

    nrow = lax.broadcasted_iota(jnp.int32, (MASK_COLS, L), 0)
    lane = lax.broadcasted_iota(jnp.int32, (L, LANES), 1)
    for g in range(G):
        q = q_ref[0, :, g * hd:(g + 1) * hd]
        km = km_ref[g]
        km_hi = km.astype(BF16)
        km_lo = (km - km_hi.astype(F32)).astype(BF16)
        gate = _dot_nt(km_hi, q) + _dot_nt(km_lo, q)
        rank = jnp.zeros(gate.shape, F32)
        for m in range(nb - 1):
            gm = gate[m:m + 1, :]
            ahead = (gm > gate) | ((gm == gate) & (nrow > m))
            rank = rank + jnp.where(ahead, 1.0, 0.0) * (qi > m).astype(F32)
        chosen = ((rank < float(MOBA_TOPK)) & (nrow < qi)) | (nrow == qi)
        bias_t = jnp.where(chosen, 0.0, MASKED)
        bias = jnp.concatenate([bias_t, jnp.zeros((LANES - MASK_COLS, L), F32)], axis=0).T
        aug = jnp.where(lane < MASK_COLS, bias, jnp.where(lane < MASK_COLS + BIAS_COLS, 1.0, 0.0))
        qa_ref[g] = jnp.concatenate([(q.astype(F32) * scale).astype(BF16), aug.astype(BF16)], axis=1)

    def scores(g, rows):
        kb = jnp.concatenate([k_ref[0, rows, g * hd:(g + 1) * hd], ka_ref[g, rows, :]], axis=1)
        return _dot_nt(qa_ref[g], kb)

    own = pl.ds(pl.multiple_of(qi * L, L), L)
    row = lax.broadcasted_iota(jnp.int32, (L, L), 0)
    col = lax.broadcasted_iota(jnp.int32, (L, L), 1)
    for g in range(G):
        s = jnp.where(col <= row, scores(g, own), MASKED)
        m0 = jnp.max(s, axis=1, keepdims=True)
        p = jnp.exp(s - m0)
        m_ref[g] = jnp.broadcast_to(m0, (L, hd))
        l_ref[g] = jnp.broadcast_to(jnp.sum(p, axis=1, keepdims=True), (L, hd))
        acc_ref[g] = jnp.dot(p.astype(BF16), v_ref[0, own, g * hd:(g + 1) * hd], preferred_element_type=F32)

    def past_block(n, carry):
        rows = pl.ds(pl.multiple_of(n * L, L), L)
        for g in range(G):
            s = scores(g, rows)
            m_prev = m_ref[g]
            m_new = jnp.maximum(m_prev, jnp.max(s, axis=1, keepdims=True))
            alpha = jnp.exp(m_prev - m_new)
            p0 = jnp.exp(s[:, :hd] - m_new)
            p1 = jnp.exp(s[:, hd:] - m_new)
            l_ref[g] = alpha * l_ref[g] + jnp.sum(p0 + p1, axis=1, keepdims=True)
            pv = jnp.dot(jnp.concatenate([p0, p1], axis=1).astype(BF16), v_ref[0, rows, g * hd:(g + 1) * hd],
                         preferred_element_type=F32)
            acc_ref[g] = alpha * acc_ref[g] + pv
            m_ref[g] = m_new
        return carry

    lax.fori_loop(0, qi, past_block, 0)

    for g in range(G):
        o_ref[0, :, g * hd:(g + 1) * hd] = (acc_ref[g] / l_ref[g]).astype(o_ref.dtype)


def _bf16_part(x):
    bits = lax.bitcast_convert_type(x, jnp.uint32) & jnp.uint32(0xFFFF0000)
    return lax.bitcast_convert_type(bits, F32)


def _moba(proj, slopes, S, col_q, col_k, col_v):
    B = proj.shape[0]
    H, G, L, hd = ATTN_HEADS, MOBA_GROUP, MOBA_BLOCK, LANES
    nb = S // L
    assert nb * L == S and nb <= MASK_COLS and H % G == 0 and col_q % G == col_k % G == col_v % G == 0
    pos = jnp.arange(S, dtype=F32)
    kbias = slopes.astype(F32)[:, None] * pos[None, :]
    hi = _bf16_part(kbias)
    mid = _bf16_part(kbias - hi)
    lo = _bf16_part(kbias - hi - mid)
    onehot = (jnp.arange(S)[:, None] // L == jnp.arange(MASK_COLS)[None, :]).astype(BF16)
    parts = [p.astype(BF16)[..., None] for p in (hi, mid, lo)]
    ka = jnp.concatenate([jnp.broadcast_to(onehot[None], (H, S, MASK_COLS))] + parts
                         + [jnp.zeros((H, S, LANES - MASK_COLS - BIAS_COLS), BF16)], axis=-1)
    return pl.pallas_call(
        functools.partial(_moba_kernel, nb=nb, scale=hd ** -0.5),
        grid=(B, H // G, nb),
        in_specs=[pl.BlockSpec((1, L, G * hd), lambda b, h, i: (b, i, col_q // G + h)),
                  pl.BlockSpec((1, S, G * hd), lambda b, h, i: (b, 0, col_k // G + h)),
                  pl.BlockSpec((1, S, G * hd), lambda b, h, i: (b, 0, col_v // G + h)),
                  pl.BlockSpec((G, S, LANES), lambda b, h, i: (h, 0, 0))],
        out_specs=pl.BlockSpec((1, L, G * hd), lambda b, h, i: (b, i, h)),
        out_shape=jax.ShapeDtypeStruct((B, S, H * hd), BF16),
        scratch_shapes=[pltpu.VMEM((G, MASK_COLS, hd), F32),
                        pltpu.VMEM((G, L, 2 * hd), BF16),
                        pltpu.VMEM((G, L, hd), F32), pltpu.VMEM((G, L, hd), F32), pltpu.VMEM((G, L, hd), F32)],
        compiler_params=_params("parallel", "parallel", "arbitrary"),
        name="moba_attention",
    )(proj, proj, proj, ka)


def _mix_kernel(attn_ref, ga_ref, cg_ref, x_ref, wao_ref, wmo_ref, o_ref):
    ao = jnp.dot(attn_ref[...], wao_ref[...], preferred_element_type=F32)
    merged = cg_ref[...].astype(F32) + jax.nn.sigmoid(ga_ref[...].astype(F32)) * ao
    o_ref[...] = x_ref[...] + jnp.dot(merged.astype(BF16), wmo_ref[...], preferred_element_type=F32)


def _mix(attn, proj2d, cg, x, w_ao, w_mo, tm, col_ga):
    T, D = x.shape
    A = attn.shape[1]
    tok = lambda c: pl.BlockSpec((tm, c), lambda i: (i, 0))
    return pl.pallas_call(
        _mix_kernel,
        grid=(T // tm,),
        in_specs=[tok(A), pl.BlockSpec((tm, D), lambda i: (i, col_ga)), tok(D), tok(D),
                  pl.BlockSpec((A, D), lambda i: (0, 0)), pl.BlockSpec((D, D), lambda i: (0, 0))],
        out_specs=tok(D),
        out_shape=jax.ShapeDtypeStruct((T, D), F32),
        compiler_params=_params("parallel"),
        name="attn_out_merge_mix",
    )(attn, proj2d, cg, x, w_ao, w_mo)


def _xattn_kernel(h_ref, g_ref, wq_ref, kv_ref, wo_ref, o_ref, oh_ref):
    D = h_ref.shape[-1]
    hd = D // XATTN_HEADS
    h = h_ref[0]
    hn = _rms(h, g_ref[...]).astype(BF16)
    q = (jnp.dot(hn, wq_ref[...], preferred_element_type=F32) * (hd ** -0.5)).astype(BF16)
    for i in range(XATTN_HEADS):
        kh = kv_ref[0, :, i * hd:(i + 1) * hd]
        vh = kv_ref[0, :, D + i * hd:D + (i + 1) * hd]
        s = _dot_nt(q[:, i * hd:(i + 1) * hd], kh)
        p = jnp.exp(s - jnp.max(s, axis=-1, keepdims=True))
        l = jnp.sum(p, axis=-1, keepdims=True)
        oh_ref[:, i * hd:(i + 1) * hd] = (jnp.dot(p.astype(BF16), vh, preferred_element_type=F32) / l).astype(BF16)
    o_ref[0] = h + jnp.dot(oh_ref[...], wo_ref[...], preferred_element_type=F32)


def _xattn(h, g, w_q, kv, w_o, tm):
    B, S, D = h.shape
    M = kv.shape[1]
    return pl.pallas_call(
        _xattn_kernel,
        grid=(B, S // tm),
        in_specs=[pl.BlockSpec((1, tm, D), lambda b, i: (b, i, 0)),
                  pl.BlockSpec((1, D), lambda b, i: (0, 0)),
                  pl.BlockSpec((D, D), lambda b, i: (0, 0)),
                  pl.BlockSpec((1, M, 2 * D), lambda b, i: (b, 0, 0)),
                  pl.BlockSpec((D, D), lambda b, i: (0, 0))],
        out_specs=pl.BlockSpec((1, tm, D), lambda b, i: (b, i, 0)),
        out_shape=jax.ShapeDtypeStruct((B, S, D), F32),
        scratch_shapes=[pltpu.VMEM((tm, D), BF16)],
        compiler_params=_params("parallel", "parallel"),
        name="memory_cross_attention",
    )(h, g.reshape(1, D), w_q, kv, w_o)


def _sort_pairs(n):
    pairs = []
    p = 1
    while p < n:
        k = p
        while k >= 1:
            for j in range(k % p, n - k, 2 * k):
                for i in range(min(k, n - j - k)):
                    if (i + j) // (2 * p) == (i + j + k) // (2 * p):
                        pairs.append((i + j, i + j + k))
            k //= 2
        p *= 2
    return pairs


def _sort_desc(xs):
    xs = list(xs)
    for i, j in _sort_pairs(len(xs)):
        hi, lo = jnp.maximum(xs[i], xs[j]), jnp.minimum(xs[i], xs[j])
        xs[i], xs[j] = hi, lo
    return xs


def _merge_top(a, b):
    n = len(a)
    xs = [jnp.maximum(a[i], b[n - 1 - i]) for i in range(n)]
    d = n // 2
    while d >= 1:
        for i in range(n):
            if (i // d) % 2 == 0:
                hi, lo = jnp.maximum(xs[i], xs[i + d]), jnp.minimum(xs[i], xs[i + d])
                xs[i], xs[i + d] = hi, lo
        d //= 2
    return xs


def _top_sorted(xs, k):
    groups = [_sort_desc(xs[i:i + k]) for i in range(0, len(xs), k)]
    while len(groups) > 1:
        groups = [_merge_top(groups[i], groups[i + 1]) if i + 1 < len(groups) else groups[i]
                  for i in range(0, len(groups), 2)]
    return groups[0]


def _peer_prep_kernel(h_ref, g_ref, wq_ref, wkh_ref, whk_ref, xnt_ref, s1_ref, s2_ref, tau_ref, skh_ref):
    K = PEER_TOPK
    Hp = PEER_HEADS
    NK = PEER_NKEYS
    W = wkh_ref.shape[-1]
    xn = _rms(h_ref[...], g_ref[...]).astype(BF16)
    xnt_ref[...] = _rms(h_ref[...], g_ref[...]).T.astype(BF16)
    q = jnp.dot(xn, wq_ref[...], preferred_element_type=F32).astype(BF16)

    tops = []
    for half in range(2):
        skh_ref[half] = _dot_nt(wkh_ref[half], q[:, half * W:(half + 1) * W]) * LOG2E
        tops.append(_top_sorted([skh_ref[half, k * Hp:(k + 1) * Hp, :] for k in range(NK)], K))
    v1, v2 = tops

    pairs = [(i, j) for i in range(K) for j in range(K) if (i + 1) * (j + 1) <= K]

    def top_sums(first):
        cand = [v2[j] + first[i] for i, j in pairs]
        cand += [jnp.full_like(cand[0], -jnp.inf)] * (-len(cand) % K)
        return _top_sorted(cand, K)

    mx = v1[0] + v2[0]
    c1 = [v - mx for v in v1]
    z = functools.reduce(lambda a, b: a + b, [jnp.exp2(t) for t in top_sums(c1)])
    lz = jnp.log(z) * LOG2E + 1.0
    tau_ref[...] = top_sums([c - lz for c in c1])[K - 1]
    for k in range(NK):
        s1_ref[k * Hp:(k + 1) * Hp, :] = (skh_ref[0, k * Hp:(k + 1) * Hp, :] - mx) - lz

    s2 = _dot_nt(whk_ref[...], q[:, W:2 * W]) * LOG2E
    for h in range(Hp):
        s2_ref[h] = s2[h * NK:(h + 1) * NK, :]


def _peer_prep(h, g, wq, wkh, whk, tm):
    T, D = h.shape
    Hp, NK = PEER_HEADS, PEER_NKEYS
    Wq = wq.shape[1]
    W = wkh.shape[-1]
    const2 = lambda shape: pl.BlockSpec(shape, lambda i: (0,) * len(shape))
    return pl.pallas_call(
        _peer_prep_kernel,
        grid=(T // tm,),
        in_specs=[pl.BlockSpec((tm, D), lambda i: (i, 0)), const2((1, D)), const2((D, Wq)),
                  const2((2, NK * Hp, W)), const2((Hp * NK, W))],
        out_specs=[pl.BlockSpec((D, tm), lambda i: (0, i)),
                   pl.BlockSpec((NK * Hp, tm), lambda i: (0, i)),
                   pl.BlockSpec((Hp, NK, tm), lambda i: (0, 0, i)),
                   pl.BlockSpec((Hp, tm), lambda i: (0, i))],
        out_shape=[jax.ShapeDtypeStruct((D, T), BF16),
                   jax.ShapeDtypeStruct((NK * Hp, T), F32),
                   jax.ShapeDtypeStruct((Hp, NK, T), F32),
                   jax.ShapeDtypeStruct((Hp, T), F32)],
        scratch_shapes=[pltpu.VMEM((2, NK * Hp, tm), F32)],
        compiler_params=_params("parallel"),
        name="peer_retrieval",
    )(h, g.reshape(1, D), wq, wkh, whk)


def _peer_expert_kernel(xnt_ref, u_ref, vt_ref, s1_ref, s2_ref, tau_ref, h_ref, g_ref, o_ref,
                        acc_ref, at0_ref, at1_ref, gt0_ref, gt1_ref, *, te, tm):
    j = pl.program_id(1)
    NK, Hp = PEER_NKEYS, PEER_HEADS

    @pl.when(j == 0)
    def _():
        acc_ref[...] = jnp.zeros(acc_ref.shape, F32)
        for ref in (at0_ref, at1_ref, gt0_ref, gt1_ref):
            ref[...] = jnp.zeros(ref.shape, ref.dtype)

    def gate_block(at_prev, gt_prev, r, c, k0):
        cs = slice(c * LANES, (c + 1) * LANES)
        ks = slice(k0, k0 + GATE_ROWS)
        w = jnp.zeros((GATE_ROWS, LANES), F32)
        for h in range(Hp):
            total = s2_ref[h, ks, cs] + s1_ref[r * Hp + h:r * Hp + h + 1, cs]
            w = w + jnp.where(total >= tau_ref[h:h + 1, cs], jnp.exp2(total), 0.0)
        rows = slice(r * NK + k0, r * NK + k0 + GATE_ROWS)
        a = at_prev[rows, cs]
        gt_prev[rows, cs] = (a * (1.0 + lax.erf(a * math.sqrt(0.5))) * w).astype(BF16)

    def step(at_cur, at_prev, gt_cur, gt_prev):
        W = MXU_WIDTH
        D = u_ref.shape[1]
        pieces = []
        for n in range(tm // W):
            ts = slice(n * W, (n + 1) * W)
            pieces += [("scores", ts, slice(k * D // 2, (k + 1) * D // 2), k) for k in range(2)]
        for n in range(tm // W):
            ts = slice(n * W, (n + 1) * W)
            pieces += [("values", ts, slice(k * W, (k + 1) * W), k) for k in range(te // W)]
        blocks = [(r, c, k0) for r in range(te // NK) for c in range(tm // LANES) for k0 in range(0, NK, GATE_ROWS)]
        per = len(blocks) // len(pieces)
        for i, (kind, ts, kk, k) in enumerate(pieces):
            if kind == "scores":
                part = jnp.dot(u_ref[:, kk], xnt_ref[kk, ts], preferred_element_type=F32)
                if k == 0:
                    at_cur[:, ts] = part
                else:
                    at_cur[:, ts] += part
            else:
                acc_ref[:, ts] += jnp.dot(vt_ref[:, kk], gt_cur[kk, ts], preferred_element_type=F32)
            for blk in blocks[i * per:(i + 1) * per]:
                gate_block(at_prev, gt_prev, *blk)

    pl.when(j % 2 == 0)(lambda: step(at0_ref, at1_ref, gt0_ref, gt1_ref))
    pl.when(j % 2 == 1)(lambda: step(at1_ref, at0_ref, gt1_ref, gt0_ref))

    @pl.when(j == pl.num_programs(1) - 1)
    def _():
        o_ref[...] = _rms(h_ref[...] + acc_ref[...].T, g_ref[...])


def _peer_experts(xnt, u, vt, s1, s2, tau, h, g, tm, te):
    T, D = h.shape
    NE = u.shape[0]
    Hp, NK = PEER_HEADS, PEER_NKEYS
    rows = te // NK * Hp
    nj = NE // te
    tile = lambda j, lag: jnp.clip(j - lag, 0, nj - 1)
    return pl.pallas_call(
        functools.partial(_peer_expert_kernel, te=te, tm=tm),
        grid=(T // tm, nj + 2),
        in_specs=[pl.BlockSpec((D, tm), lambda i, j: (0, i)),
                  pl.BlockSpec((te, D), lambda i, j: (tile(j, 0), 0)),
                  pl.BlockSpec((D, te), lambda i, j: (0, tile(j, 2))),
                  pl.BlockSpec((rows, tm), lambda i, j: (tile(j, 1), i)),
                  pl.BlockSpec((Hp, NK, tm), lambda i, j: (0, 0, i)),
                  pl.BlockSpec((Hp, tm), lambda i, j: (0, i)),
                  pl.BlockSpec((tm, D), lambda i, j: (i, 0)),
                  pl.BlockSpec((1, D), lambda i, j: (0, 0))],
        out_specs=pl.BlockSpec((tm, D), lambda i, j: (i, 0)),
        out_shape=jax.ShapeDtypeStruct((T, D), F32),
        scratch_shapes=[pltpu.VMEM((D, tm), F32), pltpu.VMEM((te, tm), F32), pltpu.VMEM((te, tm), F32),
                        pltpu.VMEM((te, tm), BF16), pltpu.VMEM((te, tm), BF16)],
        compiler_params=_params("parallel", "arbitrary"),
        name="peer_experts",
    )(xnt, u, vt, s1, s2, tau, h, g.reshape(1, D))


def _layer(h, mem, mix_norm_g, w_in, conv_dw_w, conv_dw_b, conv_ln_g, conv_ln_b, w_conv_out, b_conv_out,
           w_attn_out, w_mix_out, xattn_norm_g, mem_norm_g, w_xq, w_xkv, w_xo, ffn_norm_g, w_peer_q,
           peer_subkeys, peer_u, peer_v, out_norm_g):
    B, S, D = h.shape
    T = B * S
    M = mem.shape[1]
    A = ATTN_HEADS * LANES
    C = conv_dw_w.shape[-1]
    assert A == D and C == D, "column-block addressing of the combined projection assumes equal widths"
    Hp, NK = PEER_HEADS, PEER_NKEYS
    half = peer_subkeys.shape[-1]

    x2 = h.reshape(T, D)
    proj = _rms_matmul(x2, mix_norm_g, w_in.astype(BF16), tm=min(1024, T), tn=D)
    proj3 = proj.reshape(B, S, -1)
    cg = _conv_branch(proj3, conv_dw_w, conv_dw_b, conv_ln_g, conv_ln_b, w_conv_out.astype(BF16), b_conv_out,
                      ts=min(512, S), col_a=3, col_b=4, col_g=5)
    slopes = 2.0 ** (-8.0 * jnp.arange(1, ATTN_HEADS + 1, dtype=F32) / ATTN_HEADS)
    attn = _moba(proj3, slopes, S, col_q=0, col_k=ATTN_HEADS, col_v=2 * ATTN_HEADS)
    h1 = _mix(attn.reshape(T, A), proj, cg.reshape(T, D), x2, w_attn_out.astype(BF16), w_mix_out.astype(BF16),
              tm=min(512, T), col_ga=6)

    kv = _rms_matmul(mem.reshape(B * M, D), mem_norm_g, w_xkv.astype(BF16), tm=min(512, B * M), tn=D)
    h2 = _xattn(h1.reshape(B, S, D), xattn_norm_g, w_xq.astype(BF16), kv.reshape(B, M, 2 * D), w_xo.astype(BF16),
                tm=min(512, S)).reshape(T, D)

    wq = w_peer_q.reshape(D, Hp, 2, half).transpose(0, 2, 1, 3).reshape(D, 2 * Hp * half).astype(BF16)
    eye = jnp.eye(Hp, dtype=F32)
    sk = peer_subkeys.astype(F32)
    wkh = jnp.einsum('hpkd,hg->pkhgd', sk, eye).reshape(2, NK * Hp, Hp * half).astype(BF16)
    whk = jnp.einsum('hkd,hg->hkgd', sk[:, 1], eye).reshape(Hp * NK, Hp * half).astype(BF16)
    xnt, s1, s2, tau = _peer_prep(h2, ffn_norm_g, wq, wkh, whk, tm=min(256, T))
    return _peer_experts(xnt, peer_u.astype(BF16), peer_v.T.astype(BF16), s1, s2, tau, h2, out_norm_g,
                         tm=min(512, T), te=512)


def kernel(x, mem, mix_norm_g, w_in, conv_dw_w, conv_dw_b, conv_ln_g, conv_ln_b, w_conv_out, b_conv_out, w_attn_out, w_mix_out, xattn_norm_g, mem_norm_g, w_xq, w_xkv, w_xo, ffn_norm_g, w_peer_q, peer_subkeys, peer_u, peer_v, final_norm_g):
    depth = w_in.shape[0]
    assert depth == 1, "the last layer's kernel applies the final norm; deeper stacks need a plain-residual variant"
    B, S, D = x.shape
    out = _layer(x, mem, mix_norm_g[0], w_in[0], conv_dw_w[0], conv_dw_b[0], conv_ln_g[0], conv_ln_b[0],
                 w_conv_out[0], b_conv_out[0], w_attn_out[0], w_mix_out[0], xattn_norm_g[0], mem_norm_g[0],
                 w_xq[0], w_xkv[0], w_xo[0], ffn_norm_g[0], w_peer_q[0], peer_subkeys[0], peer_u[0], peer_v[0],
                 final_norm_g)
    return out.reshape(B, S, D)
```

```python
import functools
import math

import jax
import jax.numpy as jnp
from jax import lax
from jax.experimental import pallas as pl
from jax.experimental.pallas import tpu as pltpu

F32 = jnp.float32
BF16 = jnp.bfloat16

EPS = 1e-6
ATTN_HEADS = 8
MOBA_BLOCK = 256
MOBA_TOPK = 3
CONV_WIDTH = 31
XATTN_HEADS = 4
PEER_HEADS = 8
PEER_NKEYS = 128
PEER_TOPK = 16
LOG2E = math.log2(math.e)

LANES = 128
SUBLANES = 8
MXU_WIDTH = 256
GATE_ROWS = 32
VMEM_LIMIT = 56 * 1024 * 1024
CONV_HALO = 32


def _params(*sem):
    return pltpu.CompilerParams(dimension_semantics=sem, vmem_limit_bytes=VMEM_LIMIT)


def _dot_nt(a, b):
    return lax.dot_general(a, b, (((1,), (1,)), ((), ())), preferred_element_type=F32)


def _rms(x, g):
    return x * lax.rsqrt(jnp.mean(x * x, axis=-1, keepdims=True) + EPS) * g


def _rms_matmul_kernel(x_ref, g_ref, w_ref, o_ref, xn_ref):
    @pl.when(pl.program_id(1) == 0)
    def _():
        xn_ref[...] = _rms(x_ref[...], g_ref[...]).astype(BF16)

    o_ref[...] = jnp.dot(xn_ref[...], w_ref[...], preferred_element_type=F32).astype(o_ref.dtype)


def _rms_matmul(x, g, w, tm, tn):
    T, D = x.shape
    N = w.shape[1]
    return pl.pallas_call(
        _rms_matmul_kernel,
        grid=(T // tm, N // tn),
        in_specs=[pl.BlockSpec((tm, D), lambda i, j: (i, 0)),
                  pl.BlockSpec((1, D), lambda i, j: (0, 0)),
                  pl.BlockSpec((D, tn), lambda i, j: (0, j))],
        out_specs=pl.BlockSpec((tm, tn), lambda i, j: (i, j)),
        out_shape=jax.ShapeDtypeStruct((T, N), BF16),
        scratch_shapes=[pltpu.VMEM((tm, D), BF16)],
        compiler_params=_params("parallel", "arbitrary"),
        name="rms_in_proj",
    )(x, g.reshape(1, D), w)


def _conv_kernel(a_ref, b_ref, gc_ref, dww_ref, dwb_ref, lng_ref, lnb_ref, w_ref, bo_ref, o_ref,
                 ph_ref, act_ref, *, ts, rc):
    C = a_ref.shape[-1]
    half = C // 2
    ext = CONV_HALO + ts

    @pl.when(pl.program_id(1) == 0)
    def _():
        ph_ref[0, 0:CONV_HALO, :] = jnp.zeros((CONV_HALO, C), F32)
        ph_ref[0, ext:ext + SUBLANES, :] = jnp.zeros((SUBLANES, C), F32)

    ph_ref[0, CONV_HALO:ext, :] = a_ref[0].astype(F32) * jax.nn.sigmoid(b_ref[0].astype(F32))

    def phase_chunk(i, carry):
        r0 = pl.multiple_of(i * CONV_HALO, CONV_HALO)
        for c0 in (0, half):
            win = ph_ref[0, pl.ds(r0, CONV_HALO + SUBLANES), c0:c0 + half]
            for j in range(1, SUBLANES):
                ph_ref[j, pl.ds(r0, CONV_HALO), c0:c0 + half] = win[j:j + CONV_HALO, :]
        return carry

    lax.fori_loop(0, ext // CONV_HALO, phase_chunk, 0)

    first = CONV_HALO - (CONV_WIDTH - 1)

    def row_chunk(r, carry):
        r0 = pl.multiple_of(r * rc, rc)
        parts = []
        for c0 in (0, half):
            accs = [dwb_ref[:, c0:c0 + half]] * (rc // SUBLANES)
            for w in range(CONV_WIDTH):
                k, j = divmod(first + w, SUBLANES)
                wgt = dww_ref[w, :, c0:c0 + half]
                for g in range(rc // SUBLANES):
                    rows = pl.ds(pl.multiple_of(r0 + (k + g) * SUBLANES, SUBLANES), SUBLANES)
                    accs[g] = accs[g] + wgt * ph_ref[j, rows, c0:c0 + half]
            parts.append(jnp.concatenate(accs, axis=0))
        y = jnp.concatenate(parts, axis=1)
        mu = jnp.mean(y, axis=-1, keepdims=True)
        yc = y - mu
        var = jnp.mean(yc * yc, axis=-1, keepdims=True)
        z = yc * lax.rsqrt(var + EPS) * lng_ref[...] + lnb_ref[...]
        act_ref[pl.ds(r0, rc), :] = (z * jax.nn.sigmoid(z)).astype(BF16)
        return carry

    lax.fori_loop(0, ts // rc, row_chunk, 0)

    ph_ref[0, 0:CONV_HALO, :] = ph_ref[0, ts:ts + CONV_HALO, :]
    co = jnp.dot(act_ref[...], w_ref[...], preferred_element_type=F32) + bo_ref[...]
    o_ref[0] = (jax.nn.sigmoid(gc_ref[0].astype(F32)) * co).astype(o_ref.dtype)


def _conv_branch(proj, dw_w, dw_b, ln_g, ln_b, w_out, b_out, ts, col_a, col_b, col_g):
    B, S, _ = proj.shape
    C = w_out.shape[0]
    D = w_out.shape[1]
    vec = lambda c: pl.BlockSpec((1, c), lambda b, s: (0, 0))
    return pl.pallas_call(
        functools.partial(_conv_kernel, ts=ts, rc=32),
        grid=(B, S // ts),
        in_specs=[pl.BlockSpec((1, ts, C), lambda b, s: (b, s, col_a)),
                  pl.BlockSpec((1, ts, C), lambda b, s: (b, s, col_b)),
                  pl.BlockSpec((1, ts, D), lambda b, s: (b, s, col_g)),
                  pl.BlockSpec((CONV_WIDTH, SUBLANES, C), lambda b, s: (0, 0, 0)),
                  pl.BlockSpec((SUBLANES, C), lambda b, s: (0, 0)),
                  vec(C), vec(C),
                  pl.BlockSpec((C, D), lambda b, s: (0, 0)),
                  vec(D)],
        out_specs=pl.BlockSpec((1, ts, D), lambda b, s: (b, s, 0)),
        out_shape=jax.ShapeDtypeStruct((B, S, D), BF16),
        scratch_shapes=[pltpu.VMEM((SUBLANES, CONV_HALO + ts + SUBLANES, C), F32), pltpu.VMEM((ts, C), BF16)],
        compiler_params=_params("parallel", "arbitrary"),
        name="conv_branch",
    )(proj, proj, proj,
      jnp.broadcast_to(dw_w[:, None, :], (CONV_WIDTH, SUBLANES, C)),
      jnp.broadcast_to(dw_b[None, :], (SUBLANES, C)),
      ln_g.reshape(1, C), ln_b.reshape(1, C), w_out, b_out.reshape(1, D))


MOBA_GROUP = 4
MASK_COLS = 2 * SUBLANES
BIAS_COLS = 3
MASKED = -1e9


def _moba_kernel(q_ref, k_ref, v_ref, ka_ref, o_ref, km_ref, qa_ref, m_ref, l_ref, acc_ref, *, nb, scale):
    L = MOBA_BLOCK
    G = MOBA_GROUP
    hd = LANES
    qi = pl.program_id(2)

    @pl.when(qi == 0)
    def _():
        km_ref[...] = jnp.zeros(km_ref.shape, F32)
        for g in range(G):
            for n in range(nb):
                kb = k_ref[0, n * L:(n + 1) * L, g * hd:(g + 1) * hd].astype(F32)
                km_ref[g, n:n + 1, :] = jnp.mean(kb, axis=0, keepdims=True)

    nrow = lax.broadcasted_iota(jnp.int32, (MASK_COLS, L), 0)
    lane = lax.broadcasted_iota(jnp.int32, (L, LANES), 1)
    for g in range(G):
        q = q_ref[0, :, g * hd:(g + 1) * hd]
        km = km_ref[g]
        km_hi = km.astype(BF16)
        km_lo = (km - km_hi.astype(F32)).astype(BF16)
        gate = _dot_nt(km_hi, q) + _dot_nt(km_lo, q)
        rank = jnp.zeros(gate.shape, F32)
        for m in range(nb - 1):
            gm = gate[m:m + 1, :]
            ahead = (gm > gate) | ((gm == gate) & (nrow > m))
            rank = rank + jnp.where(ahead, 1.0, 0.0) * (qi > m).astype(F32)
        chosen = ((rank < float(MOBA_TOPK)) & (nrow < qi)) | (nrow == qi)
        bias_t = jnp.where(chosen, 0.0, MASKED)
        bias = jnp.concatenate([bias_t, jnp.zeros((LANES - MASK_COLS, L), F32)], axis=0).T
        aug = jnp.where(lane < MASK_COLS, bias, jnp.where(lane < MASK_COLS + BIAS_COLS, 1.0, 0.0))
        qa_ref[g] = jnp.concatenate([(q.astype(F32) * scale).astype(BF16), aug.astype(BF16)], axis=1)

    def scores(g, rows):
        kb = jnp.concatenate([k_ref[0, rows, g * hd:(g + 1) * hd], ka_ref[g, rows, :]], axis=1)
        return _dot_nt(qa_ref[g], kb)

    own = pl.ds(pl.multiple_of(qi * L, L), L)
    row = lax.broadcasted_iota(jnp.int32, (L, L), 0)
    col = lax.broadcasted_iota(jnp.int32, (L, L), 1)
    for g in range(G):
        s = jnp.where(col <= row, scores(g, own), MASKED)
        m0 = jnp.max(s, axis=1, keepdims=True)
        p = jnp.exp(s - m0)
        m_ref[g] = jnp.broadcast_to(m0, (L, hd))
        l_ref[g] = jnp.broadcast_to(jnp.sum(p, axis=1, keepdims=True), (L, hd))
        acc_ref[g] = jnp.dot(p.astype(BF16), v_ref[0, own, g * hd:(g + 1) * hd], preferred_element_type=F32)

    def attend(rows, width):
        for g in range(G):
            s = scores(g, rows)
            m_prev = m_ref[g]
            m_new = jnp.maximum(m_prev, jnp.max(s, axis=1, keepdims=True))
            alpha = jnp.exp(m_prev - m_new)
            ps = [jnp.exp(s[:, c:c + hd] - m_new) for c in range(0, width, hd)]
            l_ref[g] = alpha * l_ref[g] + jnp.sum(functools.reduce(lambda a, b: a + b, ps), axis=1, keepdims=True)
            pv = jnp.dot(jnp.concatenate(ps, axis=1).astype(BF16), v_ref[0, rows, g * hd:(g + 1) * hd],
                         preferred_element_type=F32)
            acc_ref[g] = alpha * acc_ref[g] + pv
            m_ref[g] = m_new

    def past_pair(i, carry):
        attend(pl.ds(pl.multiple_of(i * 2 * L, 2 * L), 2 * L), 2 * L)
        return carry

    lax.fori_loop(0, qi // 2, past_pair, 0)

    @pl.when(qi % 2 == 1)
    def _():
        attend(pl.ds(pl.multiple_of((qi - 1) * L, L), L), L)

    for g in range(G):
        o_ref[0, :, g * hd:(g + 1) * hd] = (acc_ref[g] / l_ref[g]).astype(o_ref.dtype)


def _bf16_part(x):
    bits = lax.bitcast_convert_type(x, jnp.uint32) & jnp.uint32(0xFFFF0000)
    return lax.bitcast_convert_type(bits, F32)


def _moba(proj, slopes, S, col_q, col_k, col_v):
    B = proj.shape[0]
    H, G, L, hd = ATTN_HEADS, MOBA_GROUP, MOBA_BLOCK, LANES
    nb = S // L
    assert nb * L == S and nb <= MASK_COLS and H % G == 0 and col_q % G == col_k % G == col_v % G == 0
    pos = jnp.arange(S, dtype=F32)
    kbias = slopes.astype(F32)[:, None] * pos[None, :]
    hi = _bf16_part(kbias)
    mid = _bf16_part(kbias - hi)
    lo = _bf16_part(kbias - hi - mid)
    onehot = (jnp.arange(S)[:, None] // L == jnp.arange(MASK_COLS)[None, :]).astype(BF16)
    parts = [p.astype(BF16)[..., None] for p in (hi, mid, lo)]
    ka = jnp.concatenate([jnp.broadcast_to(onehot[None], (H, S, MASK_COLS))] + parts
                         + [jnp.zeros((H, S, LANES - MASK_COLS - BIAS_COLS), BF16)], axis=-1)
    return pl.pallas_call(
        functools.partial(_moba_kernel, nb=nb, scale=hd ** -0.5),
        grid=(B, H // G, nb),
        in_specs=[pl.BlockSpec((1, L, G * hd), lambda b, h, i: (b, i, col_q // G + h)),
                  pl.BlockSpec((1, S, G * hd), lambda b, h, i: (b, 0, col_k // G + h)),
                  pl.BlockSpec((1, S, G * hd), lambda b, h, i: (b, 0, col_v // G + h)),
                  pl.BlockSpec((G, S, LANES), lambda b, h, i: (h, 0, 0))],
        out_specs=pl.BlockSpec((1, L, G * hd), lambda b, h, i: (b, i, h)),
        out_shape=jax.ShapeDtypeStruct((B, S, H * hd), BF16),
        scratch_shapes=[pltpu.VMEM((G, MASK_COLS, hd), F32),
                        pltpu.VMEM((G, L, 2 * hd), BF16),
                        pltpu.VMEM((G, L, hd), F32), pltpu.VMEM((G, L, hd), F32), pltpu.VMEM((G, L, hd), F32)],
        compiler_params=_params("parallel", "parallel", "arbitrary"),
        name="moba_attention",
    )(proj, proj, proj, ka)


def _mix_kernel(attn_ref, ga_ref, cg_ref, x_ref, wao_ref, wmo_ref, o_ref):
    ao = jnp.dot(attn_ref[...], wao_ref[...], preferred_element_type=F32)
    merged = cg_ref[...].astype(F32) + jax.nn.sigmoid(ga_ref[...].astype(F32)) * ao
    o_ref[...] = x_ref[...] + jnp.dot(merged.astype(BF16), wmo_ref[...], preferred_element_type=F32)


def _mix(attn, proj2d, cg, x, w_ao, w_mo, tm, col_ga):
    T, D = x.shape
    A = attn.shape[1]
    tok = lambda c: pl.BlockSpec((tm, c), lambda i: (i, 0))
    return pl.pallas_call(
        _mix_kernel,
        grid=(T // tm,),
        in_specs=[tok(A), pl.BlockSpec((tm, D), lambda i: (i, col_ga)), tok(D), tok(D),
                  pl.BlockSpec((A, D), lambda i: (0, 0)), pl.BlockSpec((D, D), lambda i: (0, 0))],
        out_specs=tok(D),
        out_shape=jax.ShapeDtypeStruct((T, D), F32),
        compiler_params=_params("parallel"),
        name="attn_out_merge_mix",
    )(attn, proj2d, cg, x, w_ao, w_mo)


def _xattn_kernel(h_ref, g_ref, wq_ref, kv_ref, wo_ref, o_ref, oh_ref):
    D = h_ref.shape[-1]
    hd = D // XATTN_HEADS
    h = h_ref[0]
    hn = _rms(h, g_ref[...]).astype(BF16)
    q = (jnp.dot(hn, wq_ref[...], preferred_element_type=F32) * (hd ** -0.5)).astype(BF16)
    for i in range(XATTN_HEADS):
        kh = kv_ref[0, :, i * hd:(i + 1) * hd]
        vh = kv_ref[0, :, D + i * hd:D + (i + 1) * hd]
        s = _dot_nt(q[:, i * hd:(i + 1) * hd], kh)
        p = jnp.exp(s - jnp.max(s, axis=-1, keepdims=True))
        l = jnp.sum(p, axis=-1, keepdims=True)
        oh_ref[:, i * hd:(i + 1) * hd] = (jnp.dot(p.astype(BF16), vh, preferred_element_type=F32) / l).astype(BF16)
    o_ref[0] = h + jnp.dot(oh_ref[...], wo_ref[...], preferred_element_type=F32)


def _xattn(h, g, w_q, kv, w_o, tm):
    B, S, D = h.shape
    M = kv.shape[1]
    return pl.pallas_call(
        _xattn_kernel,
        grid=(B, S // tm),
        in_specs=[pl.BlockSpec((1, tm, D), lambda b, i: (b, i, 0)),
                  pl.BlockSpec((1, D), lambda b, i: (0, 0)),
                  pl.BlockSpec((D, D), lambda b, i: (0, 0)),
                  pl.BlockSpec((1, M, 2 * D), lambda b, i: (b, 0, 0)),
                  pl.BlockSpec((D, D), lambda b, i: (0, 0))],
        out_specs=pl.BlockSpec((1, tm, D), lambda b, i: (b, i, 0)),
        out_shape=jax.ShapeDtypeStruct((B, S, D), F32),
        scratch_shapes=[pltpu.VMEM((tm, D), BF16)],
        compiler_params=_params("parallel", "parallel"),
        name="memory_cross_attention",
    )(h, g.reshape(1, D), w_q, kv, w_o)


def _sort_pairs(n):
    pairs = []
    p = 1
    while p < n:
        k = p
        while k >= 1:
            for j in range(k % p, n - k, 2 * k):
                for i in range(min(k, n - j - k)):
                    if (i + j) // (2 * p) == (i + j + k) // (2 * p):
                        pairs.append((i + j, i + j + k))
            k //= 2
        p *= 2
    return pairs


def _sort_desc(xs):
    xs = list(xs)
    for i, j in _sort_pairs(len(xs)):
        hi, lo = jnp.maximum(xs[i], xs[j]), jnp.minimum(xs[i], xs[j])
        xs[i], xs[j] = hi, lo
    return xs


def _merge_top(a, b):
    n = len(a)
    xs = [jnp.maximum(a[i], b[n - 1 - i]) for i in range(n)]
    d = n // 2
    while d >= 1:
        for i in range(n):
            if (i // d) % 2 == 0:
                hi, lo = jnp.maximum(xs[i], xs[i + d]), jnp.minimum(xs[i], xs[i + d])
                xs[i], xs[i + d] = hi, lo
        d //= 2
    return xs


def _top_sorted(xs, k):
    groups = [_sort_desc(xs[i:i + k]) for i in range(0, len(xs), k)]
    while len(groups) > 1:
        groups = [_merge_top(groups[i], groups[i + 1]) if i + 1 < len(groups) else groups[i]
                  for i in range(0, len(groups), 2)]
    return groups[0]


def _peer_prep_kernel(h_ref, g_ref, wq_ref, wkh_ref, whk_ref, xnt_ref, s1_ref, s2_ref, tau_ref, skh_ref):
    K = PEER_TOPK
    Hp = PEER_HEADS
    NK = PEER_NKEYS
    W = wkh_ref.shape[-1]
    xn = _rms(h_ref[...], g_ref[...]).astype(BF16)
    xnt_ref[...] = _rms(h_ref[...], g_ref[...]).T.astype(BF16)
    q = jnp.dot(xn, wq_ref[...], preferred_element_type=F32).astype(BF16)

    tops = []
    for half in range(2):
        skh_ref[half] = _dot_nt(wkh_ref[half], q[:, half * W:(half + 1) * W]) * LOG2E
        tops.append(_top_sorted([skh_ref[half, k * Hp:(k + 1) * Hp, :] for k in range(NK)], K))
    v1, v2 = tops

    pairs = [(i, j) for i in range(K) for j in range(K) if (i + 1) * (j + 1) <= K]

    def top_sums(first):
        cand = [v2[j] + first[i] for i, j in pairs]
        cand += [jnp.full_like(cand[0], -jnp.inf)] * (-len(cand) % K)
        return _top_sorted(cand, K)

    mx = v1[0] + v2[0]
    c1 = [v - mx for v in v1]
    z = functools.reduce(lambda a, b: a + b, [jnp.exp2(t) for t in top_sums(c1)])
    lz = jnp.log(z) * LOG2E + 1.0
    tau_ref[...] = top_sums([c - lz for c in c1])[K - 1]
    for k in range(NK):
        s1_ref[k * Hp:(k + 1) * Hp, :] = (skh_ref[0, k * Hp:(k + 1) * Hp, :] - mx) - lz

    s2 = _dot_nt(whk_ref[...], q[:, W:2 * W]) * LOG2E
    for h in range(Hp):
        s2_ref[h] = s2[h * NK:(h + 1) * NK, :]


def _peer_prep(h, g, wq, wkh, whk, tm):
    T, D = h.shape
    Hp, NK = PEER_HEADS, PEER_NKEYS
    Wq = wq.shape[1]
    W = wkh.shape[-1]
    const2 = lambda shape: pl.BlockSpec(shape, lambda i: (0,) * len(shape))
    return pl.pallas_call(
        _peer_prep_kernel,
        grid=(T // tm,),
        in_specs=[pl.BlockSpec((tm, D), lambda i: (i, 0)), const2((1, D)), const2((D, Wq)),
                  const2((2, NK * Hp, W)), const2((Hp * NK, W))],
        out_specs=[pl.BlockSpec((D, tm), lambda i: (0, i)),
                   pl.BlockSpec((NK * Hp, tm), lambda i: (0, i)),
                   pl.BlockSpec((Hp, NK, tm), lambda i: (0, 0, i)),
                   pl.BlockSpec((Hp, tm), lambda i: (0, i))],
        out_shape=[jax.ShapeDtypeStruct((D, T), BF16),
                   jax.ShapeDtypeStruct((NK * Hp, T), F32),
                   jax.ShapeDtypeStruct((Hp, NK, T), F32),
                   jax.ShapeDtypeStruct((Hp, T), F32)],
        scratch_shapes=[pltpu.VMEM((2, NK * Hp, tm), F32)],
        compiler_params=_params("parallel"),
        name="peer_retrieval",
    )(h, g.reshape(1, D), wq, wkh, whk)


def _peer_expert_kernel(xnt_ref, u_ref, vt_ref, s1_ref, s2_ref, tau_ref, h_ref, g_ref, o_ref,
                        acc_ref, at0_ref, at1_ref, gt0_ref, gt1_ref, *, te, tm, nj):
    j = pl.program_id(0)
    NK, Hp = PEER_NKEYS, PEER_HEADS

    @pl.when(j == 0)
    def _():
        acc_ref[...] = jnp.zeros(acc_ref.shape, F32)
        for ref in (at0_ref, at1_ref, gt0_ref, gt1_ref):
            ref[...] = jnp.zeros(ref.shape, ref.dtype)

    def gate_block(at_prev, gt_prev, r, c, k0):
        cs = slice(c * LANES, (c + 1) * LANES)
        ks = slice(k0, k0 + GATE_ROWS)
        w = jnp.zeros((GATE_ROWS, LANES), F32)
        for h in range(Hp):
            total = s2_ref[h, ks, cs] + s1_ref[r * Hp + h:r * Hp + h + 1, cs]
            w = w + jnp.where(total >= tau_ref[h:h + 1, cs], jnp.exp2(total), 0.0)
        rows = slice(r * NK + k0, r * NK + k0 + GATE_ROWS)
        a = at_prev[rows, cs]
        gt_prev[rows, cs] = (a * (1.0 + lax.erf(a * math.sqrt(0.5))) * w).astype(BF16)

    def step(at_cur, at_prev, gt_cur, gt_prev):
        W = MXU_WIDTH
        D = u_ref.shape[1]
        pieces = []
        for n in range(tm // W):
            ts = slice(n * W, (n + 1) * W)
            pieces += [("scores", ts, slice(k * D // 2, (k + 1) * D // 2), k) for k in range(2)]
        for n in range(tm // W):
            ts = slice(n * W, (n + 1) * W)
            pieces += [("values", ts, slice(k * W, (k + 1) * W), k) for k in range(te // W)]
        blocks = [(r, c, k0) for r in range(te // NK) for c in range(tm // LANES) for k0 in range(0, NK, GATE_ROWS)]
        per = len(blocks) // len(pieces)
        for i, (kind, ts, kk, k) in enumerate(pieces):
            if kind == "scores":
                part = jnp.dot(u_ref[:, kk], xnt_ref[kk, ts], preferred_element_type=F32)
                if k == 0:
                    at_cur[:, ts] = part
                else:
                    at_cur[:, ts] += part
            else:
                acc_ref[:, ts] += jnp.dot(vt_ref[:, kk], gt_cur[kk, ts], preferred_element_type=F32)
            for blk in blocks[i * per:(i + 1) * per]:
                gate_block(at_prev, gt_prev, *blk)

    pl.when(j % 2 == 0)(lambda: step(at0_ref, at1_ref, gt0_ref, gt1_ref))
    pl.when(j % 2 == 1)(lambda: step(at1_ref, at0_ref, gt1_ref, gt0_ref))

    @pl.when(jnp.maximum(j - 2, 0) % nj == nj - 1)
    def _():
        o_ref[...] = _rms(h_ref[...] + acc_ref[...].T, g_ref[...])
        acc_ref[...] = jnp.zeros(acc_ref.shape, F32)


def _peer_experts(xnt, u, vt, s1, s2, tau, h, g, tm, te):
    T, D = h.shape
    NE = u.shape[0]
    Hp, NK = PEER_HEADS, PEER_NKEYS
    rows = te // NK * Hp
    nj = NE // te
    steps = (T // tm) * nj
    assert nj > 1

    def tile(s, lag):
        t = jnp.clip(s - lag, 0, steps - 1)
        return t // nj, t % nj

    return pl.pallas_call(
        functools.partial(_peer_expert_kernel, te=te, tm=tm, nj=nj),
        grid=(steps + 2,),
        in_specs=[pl.BlockSpec((D, tm), lambda s: (0, tile(s, 0)[0])),
                  pl.BlockSpec((te, D), lambda s: (tile(s, 0)[1], 0)),
                  pl.BlockSpec((D, te), lambda s: (0, tile(s, 2)[1])),
                  pl.BlockSpec((rows, tm), lambda s: (tile(s, 1)[1], tile(s, 1)[0])),
                  pl.BlockSpec((Hp, NK, tm), lambda s: (0, 0, tile(s, 1)[0])),
                  pl.BlockSpec((Hp, tm), lambda s: (0, tile(s, 1)[0])),
                  pl.BlockSpec((tm, D), lambda s: (tile(s, 2)[0], 0)),
                  pl.BlockSpec((1, D), lambda s: (0, 0))],
        out_specs=pl.BlockSpec((tm, D), lambda s: (tile(s, 2)[0], 0)),
        out_shape=jax.ShapeDtypeStruct((T, D), F32),
        scratch_shapes=[pltpu.VMEM((D, tm), F32), pltpu.VMEM((te, tm), F32), pltpu.VMEM((te, tm), F32),
                        pltpu.VMEM((te, tm), BF16), pltpu.VMEM((te, tm), BF16)],
        compiler_params=_params("arbitrary"),
        name="peer_experts",
    )(xnt, u, vt, s1, s2, tau, h, g.reshape(1, D))


def _layer(h, mem, mix_norm_g, w_in, conv_dw_w, conv_dw_b, conv_ln_g, conv_ln_b, w_conv_out, b_conv_out,
           w_attn_out, w_mix_out, xattn_norm_g, mem_norm_g, w_xq, w_xkv, w_xo, ffn_norm_g, w_peer_q,
           peer_subkeys, peer_u, peer_v, out_norm_g):
    B, S, D = h.shape
    T = B * S
    M = mem.shape[1]
    A = ATTN_HEADS * LANES
    C = conv_dw_w.shape[-1]
    assert A == D and C == D, "column-block addressing of the combined projection assumes equal widths"
    Hp, NK = PEER_HEADS, PEER_NKEYS
    half = peer_subkeys.shape[-1]

    x2 = h.reshape(T, D)
    proj = _rms_matmul(x2, mix_norm_g, w_in.astype(BF16), tm=min(1024, T), tn=D)
    proj3 = proj.reshape(B, S, -1)
    cg = _conv_branch(proj3, conv_dw_w, conv_dw_b, conv_ln_g, conv_ln_b, w_conv_out.astype(BF16), b_conv_out,
                      ts=min(512, S), col_a=3, col_b=4, col_g=5)
    slopes = 2.0 ** (-8.0 * jnp.arange(1, ATTN_HEADS + 1, dtype=F32) / ATTN_HEADS)
    attn = _moba(proj3, slopes, S, col_q=0, col_k=ATTN_HEADS, col_v=2 * ATTN_HEADS)
    h1 = _mix(attn.reshape(T, A), proj, cg.reshape(T, D), x2, w_attn_out.astype(BF16), w_mix_out.astype(BF16),
              tm=min(512, T), col_ga=6)

    kv = _rms_matmul(mem.reshape(B * M, D), mem_norm_g, w_xkv.astype(BF16), tm=min(512, B * M), tn=D)
    h2 = _xattn(h1.reshape(B, S, D), xattn_norm_g, w_xq.astype(BF16), kv.reshape(B, M, 2 * D), w_xo.astype(BF16),
                tm=min(512, S)).reshape(T, D)

    wq = w_peer_q.reshape(D, Hp, 2, half).transpose(0, 2, 1, 3).reshape(D, 2 * Hp * half).astype(BF16)
    eye = jnp.eye(Hp, dtype=F32)
    sk = peer_subkeys.astype(F32)
    wkh = jnp.einsum('hpkd,hg->pkhgd', sk, eye).reshape(2, NK * Hp, Hp * half).astype(BF16)
    whk = jnp.einsum('hkd,hg->hkgd', sk[:, 1], eye).reshape(Hp * NK, Hp * half).astype(BF16)
    xnt, s1, s2, tau = _peer_prep(h2, ffn_norm_g, wq, wkh, whk, tm=min(256, T))
    return _peer_experts(xnt, peer_u.astype(BF16), peer_v.T.astype(BF16), s1, s2, tau, h2, out_norm_g,
                         tm=min(512, T), te=512)


def kernel(x, mem, mix_norm_g, w_in, conv_dw_w, conv_dw_b, conv_ln_g, conv_ln_b, w_conv_out, b_conv_out, w_attn_out, w_mix_out, xattn_norm_g, mem_norm_g, w_xq, w_xkv, w_xo, ffn_norm_g, w_peer_q, peer_subkeys, peer_u, peer_v, final_norm_g):
    depth = w_in.shape[0]
    assert depth == 1, "the last layer's kernel applies the final norm; deeper stacks need a plain-residual variant"
    B, S, D = x.shape
    out = _layer(x, mem, mix_norm_g[0], w_in[0], conv_dw_w[0], conv_dw_b[0], conv_ln_g[0], conv_ln_b[0],
                 w_conv_out[0], b_conv_out[0], w_attn_out[0], w_mix_out[0], xattn_norm_g[0], mem_norm_g[0],
                 w_xq[0], w_xkv[0], w_xo[0], ffn_norm_g[0], w_peer_q[0], peer_subkeys[0], peer_u[0], peer_v[0],
                 final_norm_g)
    return out.reshape(B, S, D)
```

```python
import functools
import math

import jax
import jax.numpy as jnp
from jax import lax
from jax.experimental import pallas as pl
from jax.experimental.pallas import tpu as pltpu

F32 = jnp.float32
BF16 = jnp.bfloat16

EPS = 1e-6
ATTN_HEADS = 8
MOBA_BLOCK = 256
MOBA_TOPK = 3
CONV_WIDTH = 31
XATTN_HEADS = 4
PEER_HEADS = 8
PEER_NKEYS = 128
PEER_TOPK = 16
LOG2E = math.log2(math.e)

LANES = 128
SUBLANES = 8
MXU_WIDTH = 256
GATE_ROWS = 32
VMEM_LIMIT = 56 * 1024 * 1024
CONV_HALO = 32


def _params(*sem):
    return pltpu.CompilerParams(dimension_semantics=sem, vmem_limit_bytes=VMEM_LIMIT)


def _dot_nt(a, b):
    return lax.dot_general(a, b, (((1,), (1,)), ((), ())), preferred_element_type=F32)


def _rms(x, g):
    return x * lax.rsqrt(jnp.mean(x * x, axis=-1, keepdims=True) + EPS) * g


def _rms_matmul_kernel(x_ref, g_ref, w_ref, o_ref, xn_ref):
    @pl.when(pl.program_id(1) == 0)
    def _():
        xn_ref[...] = _rms(x_ref[...], g_ref[...]).astype(BF16)

    o_ref[...] = jnp.dot(xn_ref[...], w_ref[...], preferred_element_type=F32).astype(o_ref.dtype)


def _rms_matmul(x, g, w, tm, tn):
    T, D = x.shape
    N = w.shape[1]
    return pl.pallas_call(
        _rms_matmul_kernel,
        grid=(T // tm, N // tn),
        in_specs=[pl.BlockSpec((tm, D), lambda i, j: (i, 0)),
                  pl.BlockSpec((1, D), lambda i, j: (0, 0)),
                  pl.BlockSpec((D, tn), lambda i, j: (0, j))],
        out_specs=pl.BlockSpec((tm, tn), lambda i, j: (i, j)),
        out_shape=jax.ShapeDtypeStruct((T, N), BF16),
        scratch_shapes=[pltpu.VMEM((tm, D), BF16)],
        compiler_params=_params("parallel", "arbitrary"),
        name="rms_in_proj",
    )(x, g.reshape(1, D), w)


def _conv_kernel(a_ref, b_ref, gc_ref, dww_ref, dwb_ref, lng_ref, lnb_ref, w_ref, bo_ref, o_ref,
                 ph_ref, act_ref, *, ts, rc):
    C = a_ref.shape[-1]
    half = C // 2
    ext = CONV_HALO + ts

    @pl.when(pl.program_id(1) == 0)
    def _():
        ph_ref[0, 0:CONV_HALO, :] = jnp.zeros((CONV_HALO, C), F32)
        ph_ref[0, ext:ext + SUBLANES, :] = jnp.zeros((SUBLANES, C), F32)

    ph_ref[0, CONV_HALO:ext, :] = a_ref[0].astype(F32) * jax.nn.sigmoid(b_ref[0].astype(F32))

    def phase_chunk(i, carry):
        r0 = pl.multiple_of(i * CONV_HALO, CONV_HALO)
        for c0 in (0, half):
            win = ph_ref[0, pl.ds(r0, CONV_HALO + SUBLANES), c0:c0 + half]
            for j in range(1, SUBLANES):
                ph_ref[j, pl.ds(r0, CONV_HALO), c0:c0 + half] = win[j:j + CONV_HALO, :]
        return carry

    lax.fori_loop(0, ext // CONV_HALO, phase_chunk, 0)

    first = CONV_HALO - (CONV_WIDTH - 1)

    def row_chunk(r, carry):
        r0 = pl.multiple_of(r * rc, rc)
        parts = []
        for c0 in (0, half):
            accs = [dwb_ref[:, c0:c0 + half]] * (rc // SUBLANES)
            for w in range(CONV_WIDTH):
                k, j = divmod(first + w, SUBLANES)
                wgt = dww_ref[w, :, c0:c0 + half]
                for g in range(rc // SUBLANES):
                    rows = pl.ds(pl.multiple_of(r0 + (k + g) * SUBLANES, SUBLANES), SUBLANES)
                    accs[g] = accs[g] + wgt * ph_ref[j, rows, c0:c0 + half]
            parts.append(jnp.concatenate(accs, axis=0))
        y = jnp.concatenate(parts, axis=1)
        mu = jnp.mean(y, axis=-1, keepdims=True)
        yc = y - mu
        var = jnp.mean(yc * yc, axis=-1, keepdims=True)
        z = yc * lax.rsqrt(var + EPS) * lng_ref[...] + lnb_ref[...]
        act_ref[pl.ds(r0, rc), :] = (z * jax.nn.sigmoid(z)).astype(BF16)
        return carry

    lax.fori_loop(0, ts // rc, row_chunk, 0)

    ph_ref[0, 0:CONV_HALO, :] = ph_ref[0, ts:ts + CONV_HALO, :]
    co = jnp.dot(act_ref[...], w_ref[...], preferred_element_type=F32) + bo_ref[...]
    o_ref[0] = (jax.nn.sigmoid(gc_ref[0].astype(F32)) * co).astype(o_ref.dtype)


def _conv_branch(proj, dw_w, dw_b, ln_g, ln_b, w_out, b_out, ts, col_a, col_b, col_g):
    B, S, _ = proj.shape
    C = w_out.shape[0]
    D = w_out.shape[1]
    vec = lambda c: pl.BlockSpec((1, c), lambda b, s: (0, 0))
    return pl.pallas_call(
        functools.partial(_conv_kernel, ts=ts, rc=32),
        grid=(B, S // ts),
        in_specs=[pl.BlockSpec((1, ts, C), lambda b, s: (b, s, col_a)),
                  pl.BlockSpec((1, ts, C), lambda b, s: (b, s, col_b)),
                  pl.BlockSpec((1, ts, D), lambda b, s: (b, s, col_g)),
                  pl.BlockSpec((CONV_WIDTH, SUBLANES, C), lambda b, s: (0, 0, 0)),
                  pl.BlockSpec((SUBLANES, C), lambda b, s: (0, 0)),
                  vec(C), vec(C),
                  pl.BlockSpec((C, D), lambda b, s: (0, 0)),
                  vec(D)],
        out_specs=pl.BlockSpec((1, ts, D), lambda b, s: (b, s, 0)),
        out_shape=jax.ShapeDtypeStruct((B, S, D), BF16),
        scratch_shapes=[pltpu.VMEM((SUBLANES, CONV_HALO + ts + SUBLANES, C), F32), pltpu.VMEM((ts, C), BF16)],
        compiler_params=_params("parallel", "arbitrary"),
        name="conv_branch",
    )(proj, proj, proj,
      jnp.broadcast_to(dw_w[:, None, :], (CONV_WIDTH, SUBLANES, C)),
      jnp.broadcast_to(dw_b[None, :], (SUBLANES, C)),
      ln_g.reshape(1, C), ln_b.reshape(1, C), w_out, b_out.reshape(1, D))


MOBA_GROUP = 4
MASK_COLS = 2 * SUBLANES
BIAS_COLS = 3
MASKED = -1e9


def _moba_kernel(q_ref, k_ref, v_ref, ka_ref, o_ref, km_ref, qa_ref, m_ref, l_ref, acc_ref, *, nb, scale):
    L = MOBA_BLOCK
    G = MOBA_GROUP
    hd = LANES
    qi = pl.program_id(2)

    @pl.when(qi == 0)
    def _():
        km_ref[...] = jnp.zeros(km_ref.shape, F32)
        for g in range(G):
            for n in range(nb):
                kb = k_ref[0, n * L:(n + 1) * L, g * hd:(g + 1) * hd].astype(F32)
                km_ref[g, n:n + 1, :] = jnp.mean(kb, axis=0, keepdims=True)

    nrow = lax.broadcasted_iota(jnp.int32, (MASK_COLS, L), 0)
    lane = lax.broadcasted_iota(jnp.int32, (L, LANES), 1)
    for g in range(G):
        q = q_ref[0, :, g * hd:(g + 1) * hd]
        km = km_ref[g]
        km_hi = km.astype(BF16)
        km_lo = (km - km_hi.astype(F32)).astype(BF16)
        gate = _dot_nt(km_hi, q) + _dot_nt(km_lo, q)
        rank = jnp.zeros(gate.shape, F32)
        for m in range(nb - 1):
            gm = gate[m:m + 1, :]
            ahead = (gm > gate) | ((gm == gate) & (nrow > m))
            rank = rank + jnp.where(ahead, 1.0, 0.0) * (qi > m).astype(F32)
        chosen = ((rank < float(MOBA_TOPK)) & (nrow < qi)) | (nrow == qi)
        bias_t = jnp.where(chosen, 0.0, MASKED)
        bias = jnp.concatenate([bias_t, jnp.zeros((LANES - MASK_COLS, L), F32)], axis=0).T
        aug = jnp.where(lane < MASK_COLS, bias, jnp.where(lane < MASK_COLS + BIAS_COLS, 1.0, 0.0))
        qa_ref[g] = jnp.concatenate([(q.astype(F32) * scale).astype(BF16), aug.astype(BF16)], axis=1)

    def scores(g, rows):
        kb = jnp.concatenate([k_ref[0, rows, g * hd:(g + 1) * hd], ka_ref[g, rows, :]], axis=1)
        return _dot_nt(qa_ref[g], kb)

    own = pl.ds(pl.multiple_of(qi * L, L), L)
    row = lax.broadcasted_iota(jnp.int32, (L, L), 0)
    col = lax.broadcasted_iota(jnp.int32, (L, L), 1)
    for g in range(G):
        s = jnp.where(col <= row, scores(g, own), MASKED)
        m0 = jnp.max(s, axis=1, keepdims=True)
        p = jnp.exp(s - m0)
        m_ref[g] = jnp.broadcast_to(m0, (L, hd))
        l_ref[g] = jnp.broadcast_to(jnp.sum(p, axis=1, keepdims=True), (L, hd))
        acc_ref[g] = jnp.dot(p.astype(BF16), v_ref[0, own, g * hd:(g + 1) * hd], preferred_element_type=F32)

    def attend(rows, width):
        for g in range(G):
            s = scores(g, rows)
            m_prev = m_ref[g]
            m_new = jnp.maximum(m_prev, jnp.max(s, axis=1, keepdims=True))
            alpha = jnp.exp(m_prev - m_new)
            ps = [jnp.exp(s[:, c:c + hd] - m_new) for c in range(0, width, hd)]
            l_ref[g] = alpha * l_ref[g] + jnp.sum(functools.reduce(lambda a, b: a + b, ps), axis=1, keepdims=True)
            pv = jnp.dot(jnp.concatenate(ps, axis=1).astype(BF16), v_ref[0, rows, g * hd:(g + 1) * hd],
                         preferred_element_type=F32)
            acc_ref[g] = alpha * acc_ref[g] + pv
            m_ref[g] = m_new

    def past_pair(i, carry):
        attend(pl.ds(pl.multiple_of(i * 2 * L, 2 * L), 2 * L), 2 * L)
        return carry

    lax.fori_loop(0, qi // 2, past_pair, 0)

    @pl.when(qi % 2 == 1)
    def _():
        attend(pl.ds(pl.multiple_of((qi - 1) * L, L), L), L)

    for g in range(G):
        o_ref[0, :, g * hd:(g + 1) * hd] = (acc_ref[g] / l_ref[g]).astype(o_ref.dtype)


def _bf16_part(x):
    bits = lax.bitcast_convert_type(x, jnp.uint32) & jnp.uint32(0xFFFF0000)
    return lax.bitcast_convert_type(bits, F32)


def _moba(proj, slopes, S, col_q, col_k, col_v):
    B = proj.shape[0]
    H, G, L, hd = ATTN_HEADS, MOBA_GROUP, MOBA_BLOCK, LANES
    nb = S // L
    assert nb * L == S and nb <= MASK_COLS and H % G == 0 and col_q % G == col_k % G == col_v % G == 0
    pos = jnp.arange(S, dtype=F32)
    kbias = slopes.astype(F32)[:, None] * pos[None, :]
    hi = _bf16_part(kbias)
    mid = _bf16_part(kbias - hi)
    lo = _bf16_part(kbias - hi - mid)
    onehot = (jnp.arange(S)[:, None] // L == jnp.arange(MASK_COLS)[None, :]).astype(BF16)
    parts = [p.astype(BF16)[..., None] for p in (hi, mid, lo)]
    ka = jnp.concatenate([jnp.broadcast_to(onehot[None], (H, S, MASK_COLS))] + parts
                         + [jnp.zeros((H, S, LANES - MASK_COLS - BIAS_COLS), BF16)], axis=-1)
    return pl.pallas_call(
        functools.partial(_moba_kernel, nb=nb, scale=hd ** -0.5),
        grid=(B, H // G, nb),
        in_specs=[pl.BlockSpec((1, L, G * hd), lambda b, h, i: (b, i, col_q // G + h)),
                  pl.BlockSpec((1, S, G * hd), lambda b, h, i: (b, 0, col_k // G + h)),
                  pl.BlockSpec((1, S, G * hd), lambda b, h, i: (b, 0, col_v // G + h)),
                  pl.BlockSpec((G, S, LANES), lambda b, h, i: (h, 0, 0))],
        out_specs=pl.BlockSpec((1, L, G * hd), lambda b, h, i: (b, i, h)),
        out_shape=jax.ShapeDtypeStruct((B, S, H * hd), BF16),
        scratch_shapes=[pltpu.VMEM((G, MASK_COLS, hd), F32),
                        pltpu.VMEM((G, L, 2 * hd), BF16),
                        pltpu.VMEM((G, L, hd), F32), pltpu.VMEM((G, L, hd), F32), pltpu.VMEM((G, L, hd), F32)],
        compiler_params=_params("parallel", "parallel", "arbitrary"),
        name="moba_attention",
    )(proj, proj, proj, ka)


def _mix_kernel(attn_ref, ga_ref, cg_ref, x_ref, wao_ref, wmo_ref, o_ref):
    ao = jnp.dot(attn_ref[...], wao_ref[...], preferred_element_type=F32)
    merged = cg_ref[...].astype(F32) + jax.nn.sigmoid(ga_ref[...].astype(F32)) * ao
    o_ref[...] = x_ref[...] + jnp.dot(merged.astype(BF16), wmo_ref[...], preferred_element_type=F32)


def _mix(attn, proj2d, cg, x, w_ao, w_mo, tm, col_ga):
    T, D = x.shape
    A = attn.shape[1]
    tok = lambda c: pl.BlockSpec((tm, c), lambda i: (i, 0))
    return pl.pallas_call(
        _mix_kernel,
        grid=(T // tm,),
        in_specs=[tok(A), pl.BlockSpec((tm, D), lambda i: (i, col_ga)), tok(D), tok(D),
                  pl.BlockSpec((A, D), lambda i: (0, 0)), pl.BlockSpec((D, D), lambda i: (0, 0))],
        out_specs=tok(D),
        out_shape=jax.ShapeDtypeStruct((T, D), F32),
        compiler_params=_params("parallel"),
        name="attn_out_merge_mix",
    )(attn, proj2d, cg, x, w_ao, w_mo)


def _xattn_kernel(h_ref, g_ref, wq_ref, kv_ref, wo_ref, o_ref, oh_ref):
    D = h_ref.shape[-1]
    hd = D // XATTN_HEADS
    h = h_ref[0]
    hn = _rms(h, g_ref[...]).astype(BF16)
    q = (jnp.dot(hn, wq_ref[...], preferred_element_type=F32) * (hd ** -0.5)).astype(BF16)
    for i in range(XATTN_HEADS):
        kh = kv_ref[0, :, i * hd:(i + 1) * hd]
        vh = kv_ref[0, :, D + i * hd:D + (i + 1) * hd]
        s = _dot_nt(q[:, i * hd:(i + 1) * hd], kh)
        p = jnp.exp(s - jnp.max(s, axis=-1, keepdims=True))
        l = jnp.sum(p, axis=-1, keepdims=True)
        oh_ref[:, i * hd:(i + 1) * hd] = (jnp.dot(p.astype(BF16), vh, preferred_element_type=F32) / l).astype(BF16)
    o_ref[0] = h + jnp.dot(oh_ref[...], wo_ref[...], preferred_element_type=F32)


def _xattn(h, g, w_q, kv, w_o, tm):
    B, S, D = h.shape
    M = kv.shape[1]
    return pl.pallas_call(
        _xattn_kernel,
        grid=(B, S // tm),
        in_specs=[pl.BlockSpec((1, tm, D), lambda b, i: (b, i, 0)),
                  pl.BlockSpec((1, D), lambda b, i: (0, 0)),
                  pl.BlockSpec((D, D), lambda b, i: (0, 0)),
                  pl.BlockSpec((1, M, 2 * D), lambda b, i: (b, 0, 0)),
                  pl.BlockSpec((D, D), lambda b, i: (0, 0))],
        out_specs=pl.BlockSpec((1, tm, D), lambda b, i: (b, i, 0)),
        out_shape=jax.ShapeDtypeStruct((B, S, D), F32),
        scratch_shapes=[pltpu.VMEM((tm, D), BF16)],
        compiler_params=_params("parallel", "parallel"),
        name="memory_cross_attention",
    )(h, g.reshape(1, D), w_q, kv, w_o)


def _sort_pairs(n):
    pairs = []
    p = 1
    while p < n:
        k = p
        while k >= 1:
            for j in range(k % p, n - k, 2 * k):
                for i in range(min(k, n - j - k)):
                    if (i + j) // (2 * p) == (i + j + k) // (2 * p):
                        pairs.append((i + j, i + j + k))
            k //= 2
        p *= 2
    return pairs


def _sort_desc(xs):
    xs = list(xs)
    for i, j in _sort_pairs(len(xs)):
        hi, lo = jnp.maximum(xs[i], xs[j]), jnp.minimum(xs[i], xs[j])
        xs[i], xs[j] = hi, lo
    return xs


def _merge_top(a, b):
    n = len(a)
    xs = [jnp.maximum(a[i], b[n - 1 - i]) for i in range(n)]
    d = n // 2
    while d >= 1:
        for i in range(n):
            if (i // d) % 2 == 0:
                hi, lo = jnp.maximum(xs[i], xs[i + d]), jnp.minimum(xs[i], xs[i + d])
                xs[i], xs[i + d] = hi, lo
        d //= 2
    return xs


def _top_sorted(xs, k):
    groups = [_sort_desc(xs[i:i + k]) for i in range(0, len(xs), k)]
    while len(groups) > 1:
        groups = [_merge_top(groups[i], groups[i + 1]) if i + 1 < len(groups) else groups[i]
                  for i in range(0, len(groups), 2)]
    return groups[0]


def _peer_prep_kernel(h_ref, g_ref, wq_ref, wkh_ref, whk_ref, xnt_ref, s1_ref, s2_ref, tau_ref, skh_ref):
    K = PEER_TOPK
    Hp = PEER_HEADS
    NK = PEER_NKEYS
    W = wkh_ref.shape[-1]
    xn = _rms(h_ref[...], g_ref[...]).astype(BF16)
    xnt_ref[...] = _rms(h_ref[...], g_ref[...]).T.astype(BF16)
    q = jnp.dot(xn, wq_ref[...], preferred_element_type=F32).astype(BF16)

    tops = []
    for half in range(2):
        skh_ref[half] = _dot_nt(wkh_ref[half], q[:, half * W:(half + 1) * W]) * LOG2E
        tops.append(_top_sorted([skh_ref[half, k * Hp:(k + 1) * Hp, :] for k in range(NK)], K))
    v1, v2 = tops

    pairs = [(i, j) for i in range(K) for j in range(K) if (i + 1) * (j + 1) <= K]

    def top_sums(first):
        cand = [v2[j] + first[i] for i, j in pairs]
        cand += [jnp.full_like(cand[0], -jnp.inf)] * (-len(cand) % K)
        return _top_sorted(cand, K)

    mx = v1[0] + v2[0]
    c1 = [v - mx for v in v1]
    z = functools.reduce(lambda a, b: a + b, [jnp.exp2(t) for t in top_sums(c1)])
    lz = jnp.log(z) * LOG2E + 1.0
    tau_ref[...] = top_sums([c - lz for c in c1])[K - 1]
    for k in range(NK):
        s1_ref[k * Hp:(k + 1) * Hp, :] = (skh_ref[0, k * Hp:(k + 1) * Hp, :] - mx) - lz

    s2 = _dot_nt(whk_ref[...], q[:, W:2 * W]) * LOG2E
    for h in range(Hp):
        s2_ref[h] = s2[h * NK:(h + 1) * NK, :]


def _peer_prep(h, g, wq, wkh, whk, tm):
    T, D = h.shape
    Hp, NK = PEER_HEADS, PEER_NKEYS
    Wq = wq.shape[1]
    W = wkh.shape[-1]
    const2 = lambda shape: pl.BlockSpec(shape, lambda i: (0,) * len(shape))
    return pl.pallas_call(
        _peer_prep_kernel,
        grid=(T // tm,),
        in_specs=[pl.BlockSpec((tm, D), lambda i: (i, 0)), const2((1, D)), const2((D, Wq)),
                  const2((2, NK * Hp, W)), const2((Hp * NK, W))],
        out_specs=[pl.BlockSpec((D, tm), lambda i: (0, i)),
                   pl.BlockSpec((NK * Hp, tm), lambda i: (0, i)),
                   pl.BlockSpec((Hp, NK, tm), lambda i: (0, 0, i)),
                   pl.BlockSpec((Hp, tm), lambda i: (0, i))],
        out_shape=[jax.ShapeDtypeStruct((D, T), BF16),
                   jax.ShapeDtypeStruct((NK * Hp, T), F32),
                   jax.ShapeDtypeStruct((Hp, NK, T), F32),
                   jax.ShapeDtypeStruct((Hp, T), F32)],
        scratch_shapes=[pltpu.VMEM((2, NK * Hp, tm), F32)],
        compiler_params=_params("parallel"),
        name="peer_retrieval",
    )(h, g.reshape(1, D), wq, wkh, whk)


def _peer_expert_kernel(xnt_ref, u_ref, vt_ref, s1_ref, s2_ref, tau_ref, h_ref, g_ref, o_ref,
                        acc_ref, at0_ref, at1_ref, gt0_ref, gt1_ref, *, te, tm, nj):
    j = pl.program_id(0)
    NK, Hp = PEER_NKEYS, PEER_HEADS

    @pl.when(j == 0)
    def _():
        acc_ref[...] = jnp.zeros(acc_ref.shape, F32)
        for ref in (at0_ref, at1_ref, gt0_ref, gt1_ref):
            ref[...] = jnp.zeros(ref.shape, ref.dtype)

    def gate_block(at_prev, gt_prev, r, c, k0):
        cs = slice(c * LANES, (c + 1) * LANES)
        ks = slice(k0, k0 + GATE_ROWS)
        w = jnp.zeros((GATE_ROWS, LANES), F32)
        for h in range(Hp):
            total = s2_ref[h, ks, cs] + s1_ref[r * Hp + h:r * Hp + h + 1, cs]
            w = w + jnp.where(total >= tau_ref[h:h + 1, cs], jnp.exp2(total), 0.0)
        rows = slice(r * NK + k0, r * NK + k0 + GATE_ROWS)
        a = at_prev[rows, cs]
        gt_prev[rows, cs] = (a * (1.0 + lax.erf(a * math.sqrt(0.5))) * w).astype(BF16)

    def step(at_cur, at_prev, gt_cur, gt_prev):
        W = MXU_WIDTH
        D = u_ref.shape[1]
        pieces = []
        for n in range(tm // W):
            ts = slice(n * W, (n + 1) * W)
            pieces += [("scores", ts, slice(k * D // 2, (k + 1) * D // 2), k) for k in range(2)]
        for n in range(tm // W):
            ts = slice(n * W, (n + 1) * W)
            pieces += [("values", ts, slice(k * W, (k + 1) * W), k) for k in range(te // W)]
        blocks = [(r, c, k0) for r in range(te // NK) for c in range(tm // LANES) for k0 in range(0, NK, GATE_ROWS)]
        per = len(blocks) // len(pieces)
        for i, (kind, ts, kk, k) in enumerate(pieces):
            if kind == "scores":
                part = jnp.dot(u_ref[:, kk], xnt_ref[kk, ts], preferred_element_type=F32)
                if k == 0:
                    at_cur[:, ts] = part
                else:
                    at_cur[:, ts] += part
            else:
                acc_ref[:, ts] += jnp.dot(vt_ref[0, :, kk], gt_cur[kk, ts], preferred_element_type=F32)
            for blk in blocks[i * per:(i + 1) * per]:
                gate_block(at_prev, gt_prev, *blk)

    pl.when(j % 2 == 0)(lambda: step(at0_ref, at1_ref, gt0_ref, gt1_ref))
    pl.when(j % 2 == 1)(lambda: step(at1_ref, at0_ref, gt1_ref, gt0_ref))

    @pl.when(jnp.maximum(j - 2, 0) % nj == nj - 1)
    def _():
        o_ref[...] = _rms(h_ref[...] + acc_ref[...].T, g_ref[...])
        acc_ref[...] = jnp.zeros(acc_ref.shape, F32)


def _peer_experts(xnt, u, v, s1, s2, tau, h, g, tm, te):
    T, D = h.shape
    NE = u.shape[0]
    Hp, NK = PEER_HEADS, PEER_NKEYS
    rows = te // NK * Hp
    nj = NE // te
    steps = (T // tm) * nj
    assert nj > 1
    vt = v.reshape(nj, te, D).transpose(0, 2, 1)

    def tile(s, lag):
        t = jnp.clip(s - lag, 0, steps - 1)
        return t // nj, t % nj

    return pl.pallas_call(
        functools.partial(_peer_expert_kernel, te=te, tm=tm, nj=nj),
        grid=(steps + 2,),
        in_specs=[pl.BlockSpec((D, tm), lambda s: (0, tile(s, 0)[0])),
                  pl.BlockSpec((te, D), lambda s: (tile(s, 0)[1], 0)),
                  pl.BlockSpec((1, D, te), lambda s: (tile(s, 2)[1], 0, 0)),
                  pl.BlockSpec((rows, tm), lambda s: (tile(s, 1)[1], tile(s, 1)[0])),
                  pl.BlockSpec((Hp, NK, tm), lambda s: (0, 0, tile(s, 1)[0])),
                  pl.BlockSpec((Hp, tm), lambda s: (0, tile(s, 1)[0])),
                  pl.BlockSpec((tm, D), lambda s: (tile(s, 2)[0], 0)),
                  pl.BlockSpec((1, D), lambda s: (0, 0))],
        out_specs=pl.BlockSpec((tm, D), lambda s: (tile(s, 2)[0], 0)),
        out_shape=jax.ShapeDtypeStruct((T, D), F32),
        scratch_shapes=[pltpu.VMEM((D, tm), F32), pltpu.VMEM((te, tm), F32), pltpu.VMEM((te, tm), F32),
                        pltpu.VMEM((te, tm), BF16), pltpu.VMEM((te, tm), BF16)],
        compiler_params=_params("arbitrary"),
        name="peer_experts",
    )(xnt, u, vt, s1, s2, tau, h, g.reshape(1, D))


def _layer(h, mem, mix_norm_g, w_in, conv_dw_w, conv_dw_b, conv_ln_g, conv_ln_b, w_conv_out, b_conv_out,
           w_attn_out, w_mix_out, xattn_norm_g, mem_norm_g, w_xq, w_xkv, w_xo, ffn_norm_g, w_peer_q,
           peer_subkeys, peer_u, peer_v, out_norm_g):
    B, S, D = h.shape
    T = B * S
    M = mem.shape[1]
    A = ATTN_HEADS * LANES
    C = conv_dw_w.shape[-1]
    assert A == D and C == D, "column-block addressing of the combined projection assumes equal widths"
    Hp, NK = PEER_HEADS, PEER_NKEYS
    half = peer_subkeys.shape[-1]

    x2 = h.reshape(T, D)
    proj = _rms_matmul(x2, mix_norm_g, w_in.astype(BF16), tm=min(1024, T), tn=D)
    proj3 = proj.reshape(B, S, -1)
    cg = _conv_branch(proj3, conv_dw_w, conv_dw_b, conv_ln_g, conv_ln_b, w_conv_out.astype(BF16), b_conv_out,
                      ts=min(512, S), col_a=3, col_b=4, col_g=5)
    slopes = 2.0 ** (-8.0 * jnp.arange(1, ATTN_HEADS + 1, dtype=F32) / ATTN_HEADS)
    attn = _moba(proj3, slopes, S, col_q=0, col_k=ATTN_HEADS, col_v=2 * ATTN_HEADS)
    h1 = _mix(attn.reshape(T, A), proj, cg.reshape(T, D), x2, w_attn_out.astype(BF16), w_mix_out.astype(BF16),
              tm=min(512, T), col_ga=6)

    kv = _rms_matmul(mem.reshape(B * M, D), mem_norm_g, w_xkv.astype(BF16), tm=min(512, B * M), tn=D)
    h2 = _xattn(h1.reshape(B, S, D), xattn_norm_g, w_xq.astype(BF16), kv.reshape(B, M, 2 * D), w_xo.astype(BF16),
                tm=min(512, S)).reshape(T, D)

    wq = w_peer_q.reshape(D, Hp, 2, half).transpose(0, 2, 1, 3).reshape(D, 2 * Hp * half).astype(BF16)
    eye = jnp.eye(Hp, dtype=F32)
    sk = peer_subkeys.astype(F32)
    wkh = jnp.einsum('hpkd,hg->pkhgd', sk, eye).reshape(2, NK * Hp, Hp * half).astype(BF16)
    whk = jnp.einsum('hkd,hg->hkgd', sk[:, 1], eye).reshape(Hp * NK, Hp * half).astype(BF16)
    xnt, s1, s2, tau = _peer_prep(h2, ffn_norm_g, wq, wkh, whk, tm=min(256, T))
    return _peer_experts(xnt, peer_u.astype(BF16), peer_v.astype(BF16), s1, s2, tau, h2, out_norm_g,
                         tm=min(512, T), te=512)


def kernel(x, mem, mix_norm_g, w_in, conv_dw_w, conv_dw_b, conv_ln_g, conv_ln_b, w_conv_out, b_conv_out, w_attn_out, w_mix_out, xattn_norm_g, mem_norm_g, w_xq, w_xkv, w_xo, ffn_norm_g, w_peer_q, peer_subkeys, peer_u, peer_v, final_norm_g):
    depth = w_in.shape[0]
    assert depth == 1, "the last layer's kernel applies the final norm; deeper stacks need a plain-residual variant"
    B, S, D = x.shape
    out = _layer(x, mem, mix_norm_g[0], w_in[0], conv_dw_w[0], conv_dw_b[0], conv_ln_g[0], conv_ln_b[0],
                 w_conv_out[0], b_conv_out[0], w_attn_out[0], w_mix_out[0], xattn_norm_g[0], mem_norm_g[0],
                 w_xq[0], w_xkv[0], w_xo[0], ffn_norm_g[0], w_peer_q[0], peer_subkeys[0], peer_u[0], peer_v[0],
                 final_norm_g)
    return out.reshape(B, S, D)
```

```python
import functools
import math

import jax
import jax.numpy as jnp
from jax import lax
from jax.experimental import pallas as pl
from jax.experimental.pallas import tpu as pltpu

F32 = jnp.float32
BF16 = jnp.bfloat16

EPS = 1e-6
ATTN_HEADS = 8
MOBA_BLOCK = 256
MOBA_TOPK = 3
CONV_WIDTH = 31
XATTN_HEADS = 4
PEER_HEADS = 8
PEER_NKEYS = 128
PEER_TOPK = 16
LOG2E = math.log2(math.e)

LANES = 128
SUBLANES = 8
MXU_WIDTH = 256
GATE_ROWS = 32
VMEM_LIMIT = 56 * 1024 * 1024
CONV_HALO = 32


def _params(*sem):
    return pltpu.CompilerParams(dimension_semantics=sem, vmem_limit_bytes=VMEM_LIMIT)


def _dot_nt(a, b):
    return lax.dot_general(a, b, (((1,), (1,)), ((), ())), preferred_element_type=F32)


def _rms(x, g):
    return x * lax.rsqrt(jnp.mean(x * x, axis=-1, keepdims=True) + EPS) * g


def _rms_matmul_kernel(x_ref, g_ref, w_ref, o_ref, xn_ref):
    @pl.when(pl.program_id(1) == 0)
    def _():
        xn_ref[...] = _rms(x_ref[...], g_ref[...]).astype(BF16)

    o_ref[...] = jnp.dot(xn_ref[...], w_ref[...], preferred_element_type=F32).astype(o_ref.dtype)


def _rms_matmul(x, g, w, tm, tn):
    T, D = x.shape
    N = w.shape[1]
    return pl.pallas_call(
        _rms_matmul_kernel,
        grid=(T // tm, N // tn),
        in_specs=[pl.BlockSpec((tm, D), lambda i, j: (i, 0)),
                  pl.BlockSpec((1, D), lambda i, j: (0, 0)),
                  pl.BlockSpec((D, tn), lambda i, j: (0, j))],
        out_specs=pl.BlockSpec((tm, tn), lambda i, j: (i, j)),
        out_shape=jax.ShapeDtypeStruct((T, N), BF16),
        scratch_shapes=[pltpu.VMEM((tm, D), BF16)],
        compiler_params=_params("parallel", "arbitrary"),
        name="rms_in_proj",
    )(x, g.reshape(1, D), w)


def _conv_kernel(a_ref, b_ref, gc_ref, dww_ref, dwb_ref, lng_ref, lnb_ref, w_ref, bo_ref, o_ref,
                 ph_ref, act_ref, *, ts, rc):
    C = a_ref.shape[-1]
    half = C // 2
    ext = CONV_HALO + ts

    @pl.when(pl.program_id(1) == 0)
    def _():
        ph_ref[0, 0:CONV_HALO, :] = jnp.zeros((CONV_HALO, C), F32)
        ph_ref[0, ext:ext + SUBLANES, :] = jnp.zeros((SUBLANES, C), F32)

    ph_ref[0, CONV_HALO:ext, :] = a_ref[0].astype(F32) * jax.nn.sigmoid(b_ref[0].astype(F32))

    def phase_chunk(i, carry):
        r0 = pl.multiple_of(i * CONV_HALO, CONV_HALO)
        for c0 in (0, half):
            win = ph_ref[0, pl.ds(r0, CONV_HALO + SUBLANES), c0:c0 + half]
            for j in range(1, SUBLANES):
                ph_ref[j, pl.ds(r0, CONV_HALO), c0:c0 + half] = win[j:j + CONV_HALO, :]
        return carry

    lax.fori_loop(0, ext // CONV_HALO, phase_chunk, 0)

    first = CONV_HALO - (CONV_WIDTH - 1)

    def row_chunk(r, carry):
        r0 = pl.multiple_of(r * rc, rc)
        parts = []
        for c0 in (0, half):
            accs = [dwb_ref[:, c0:c0 + half]] * (rc // SUBLANES)
            for w in range(CONV_WIDTH):
                k, j = divmod(first + w, SUBLANES)
                wgt = dww_ref[w, :, c0:c0 + half]
                for g in range(rc // SUBLANES):
                    rows = pl.ds(pl.multiple_of(r0 + (k + g) * SUBLANES, SUBLANES), SUBLANES)
                    accs[g] = accs[g] + wgt * ph_ref[j, rows, c0:c0 + half]
            parts.append(jnp.concatenate(accs, axis=0))
        y = jnp.concatenate(parts, axis=1)
        mu = jnp.mean(y, axis=-1, keepdims=True)
        yc = y - mu
        var = jnp.mean(yc * yc, axis=-1, keepdims=True)
        z = yc * lax.rsqrt(var + EPS) * lng_ref[...] + lnb_ref[...]
        act_ref[pl.ds(r0, rc), :] = (z * jax.nn.sigmoid(z)).astype(BF16)
        return carry

    lax.fori_loop(0, ts // rc, row_chunk, 0)

    ph_ref[0, 0:CONV_HALO, :] = ph_ref[0, ts:ts + CONV_HALO, :]
    co = jnp.dot(act_ref[...], w_ref[...], preferred_element_type=F32) + bo_ref[...]
    o_ref[0] = (jax.nn.sigmoid(gc_ref[0].astype(F32)) * co).astype(o_ref.dtype)


def _conv_branch(proj, dw_w, dw_b, ln_g, ln_b, w_out, b_out, ts, col_a, col_b, col_g):
    B, S, _ = proj.shape
    C = w_out.shape[0]
    D = w_out.shape[1]
    vec = lambda c: pl.BlockSpec((1, c), lambda b, s: (0, 0))
    return pl.pallas_call(
        functools.partial(_conv_kernel, ts=ts, rc=32),
        grid=(B, S // ts),
        in_specs=[pl.BlockSpec((1, ts, C), lambda b, s: (b, s, col_a)),
                  pl.BlockSpec((1, ts, C), lambda b, s: (b, s, col_b)),
                  pl.BlockSpec((1, ts, D), lambda b, s: (b, s, col_g)),
                  pl.BlockSpec((CONV_WIDTH, SUBLANES, C), lambda b, s: (0, 0, 0)),
                  pl.BlockSpec((SUBLANES, C), lambda b, s: (0, 0)),
                  vec(C), vec(C),
                  pl.BlockSpec((C, D), lambda b, s: (0, 0)),
                  vec(D)],
        out_specs=pl.BlockSpec((1, ts, D), lambda b, s: (b, s, 0)),
        out_shape=jax.ShapeDtypeStruct((B, S, D), BF16),
        scratch_shapes=[pltpu.VMEM((SUBLANES, CONV_HALO + ts + SUBLANES, C), F32), pltpu.VMEM((ts, C), BF16)],
        compiler_params=_params("parallel", "arbitrary"),
        name="conv_branch",
    )(proj, proj, proj,
      jnp.broadcast_to(dw_w[:, None, :], (CONV_WIDTH, SUBLANES, C)),
      jnp.broadcast_to(dw_b[None, :], (SUBLANES, C)),
      ln_g.reshape(1, C), ln_b.reshape(1, C), w_out, b_out.reshape(1, D))


MOBA_GROUP = 4
MASK_COLS = 2 * SUBLANES
BIAS_COLS = 3
MASKED = -1e9


def _moba_kernel(q_ref, k_ref, v_ref, ka_ref, o_ref, km_ref, qa_ref, m_ref, l_ref, acc_ref, *, nb, scale):
    L = MOBA_BLOCK
    G = MOBA_GROUP
    hd = LANES
    qi = pl.program_id(2)

    @pl.when(qi == 0)
    def _():
        km_ref[...] = jnp.zeros(km_ref.shape, F32)
        for g in range(G):
            for n in range(nb):
                kb = k_ref[0, n * L:(n + 1) * L, g * hd:(g + 1) * hd].astype(F32)
                km_ref[g, n:n + 1, :] = jnp.mean(kb, axis=0, keepdims=True)

    nrow = lax.broadcasted_iota(jnp.int32, (MASK_COLS, L), 0)
    lane = lax.broadcasted_iota(jnp.int32, (L, LANES), 1)
    for g in range(G):
        q = q_ref[0, :, g * hd:(g + 1) * hd]
        km = km_ref[g]
        km_hi = km.astype(BF16)
        km_lo = (km - km_hi.astype(F32)).astype(BF16)
        gate = _dot_nt(km_hi, q) + _dot_nt(km_lo, q)
        rank = jnp.zeros(gate.shape, F32)
        for m in range(nb - 1):
            gm = gate[m:m + 1, :]
            ahead = (gm > gate) | ((gm == gate) & (nrow > m))
            rank = rank + jnp.where(ahead, 1.0, 0.0) * (qi > m).astype(F32)
        chosen = ((rank < float(MOBA_TOPK)) & (nrow < qi)) | (nrow == qi)
        bias_t = jnp.where(chosen, 0.0, MASKED)
        bias = jnp.concatenate([bias_t, jnp.zeros((LANES - MASK_COLS, L), F32)], axis=0).T
        aug = jnp.where(lane < MASK_COLS, bias, jnp.where(lane < MASK_COLS + BIAS_COLS, 1.0, 0.0))
        qa_ref[g] = jnp.concatenate([(q.astype(F32) * scale).astype(BF16), aug.astype(BF16)], axis=1)

    def scores(g, rows):
        kb = jnp.concatenate([k_ref[0, rows, g * hd:(g + 1) * hd], ka_ref[g, rows, :]], axis=1)
        return _dot_nt(qa_ref[g], kb)

    own = pl.ds(pl.multiple_of(qi * L, L), L)
    row = lax.broadcasted_iota(jnp.int32, (L, L), 0)
    col = lax.broadcasted_iota(jnp.int32, (L, L), 1)
    for g in range(G):
        s = jnp.where(col <= row, scores(g, own), MASKED)
        m0 = jnp.max(s, axis=1, keepdims=True)
        p = jnp.exp(s - m0)
        m_ref[g] = jnp.broadcast_to(m0, (L, hd))
        l_ref[g] = jnp.broadcast_to(jnp.sum(p, axis=1, keepdims=True), (L, hd))
        acc_ref[g] = jnp.dot(p.astype(BF16), v_ref[0, own, g * hd:(g + 1) * hd], preferred_element_type=F32)

    def attend(rows, width):
        for g in range(G):
            s = scores(g, rows)
            m_prev = m_ref[g]
            m_new = jnp.maximum(m_prev, jnp.max(s, axis=1, keepdims=True))
            alpha = jnp.exp(m_prev - m_new)
            ps = [jnp.exp(s[:, c:c + hd] - m_new) for c in range(0, width, hd)]
            l_ref[g] = alpha * l_ref[g] + jnp.sum(functools.reduce(lambda a, b: a + b, ps), axis=1, keepdims=True)
            pv = jnp.dot(jnp.concatenate(ps, axis=1).astype(BF16), v_ref[0, rows, g * hd:(g + 1) * hd],
                         preferred_element_type=F32)
            acc_ref[g] = alpha * acc_ref[g] + pv
            m_ref[g] = m_new

    def past_pair(i, carry):
        attend(pl.ds(pl.multiple_of(i * 2 * L, 2 * L), 2 * L), 2 * L)
        return carry

    lax.fori_loop(0, qi // 2, past_pair, 0)

    @pl.when(qi % 2 == 1)
    def _():
        attend(pl.ds(pl.multiple_of((qi - 1) * L, L), L), L)

    for g in range(G):
        o_ref[0, :, g * hd:(g + 1) * hd] = (acc_ref[g] / l_ref[g]).astype(o_ref.dtype)


def _bf16_part(x):
    bits = lax.bitcast_convert_type(x, jnp.uint32) & jnp.uint32(0xFFFF0000)
    return lax.bitcast_convert_type(bits, F32)


def _moba(proj, slopes, S, col_q, col_k, col_v):
    B = proj.shape[0]
    H, G, L, hd = ATTN_HEADS, MOBA_GROUP, MOBA_BLOCK, LANES
    nb = S // L
    assert nb * L == S and nb <= MASK_COLS and H % G == 0 and col_q % G == col_k % G == col_v % G == 0
    pos = jnp.arange(S, dtype=F32)
    kbias = slopes.astype(F32)[:, None] * pos[None, :]
    hi = _bf16_part(kbias)
    mid = _bf16_part(kbias - hi)
    lo = _bf16_part(kbias - hi - mid)
    onehot = (jnp.arange(S)[:, None] // L == jnp.arange(MASK_COLS)[None, :]).astype(BF16)
    parts = [p.astype(BF16)[..., None] for p in (hi, mid, lo)]
    ka = jnp.concatenate([jnp.broadcast_to(onehot[None], (H, S, MASK_COLS))] + parts
                         + [jnp.zeros((H, S, LANES - MASK_COLS - BIAS_COLS), BF16)], axis=-1)
    return pl.pallas_call(
        functools.partial(_moba_kernel, nb=nb, scale=hd ** -0.5),
        grid=(B, H // G, nb),
        in_specs=[pl.BlockSpec((1, L, G * hd), lambda b, h, i: (b, i, col_q // G + h)),
                  pl.BlockSpec((1, S, G * hd), lambda b, h, i: (b, 0, col_k // G + h)),
                  pl.BlockSpec((1, S, G * hd), lambda b, h, i: (b, 0, col_v // G + h)),
                  pl.BlockSpec((G, S, LANES), lambda b, h, i: (h, 0, 0))],
        out_specs=pl.BlockSpec((1, L, G * hd), lambda b, h, i: (b, i, h)),
        out_shape=jax.ShapeDtypeStruct((B, S, H * hd), BF16),
        scratch_shapes=[pltpu.VMEM((G, MASK_COLS, hd), F32),
                        pltpu.VMEM((G, L, 2 * hd), BF16),
                        pltpu.VMEM((G, L, hd), F32), pltpu.VMEM((G, L, hd), F32), pltpu.VMEM((G, L, hd), F32)],
        compiler_params=_params("parallel", "parallel", "arbitrary"),
        name="moba_attention",
    )(proj, proj, proj, ka)


def _mix_kernel(attn_ref, ga_ref, cg_ref, x_ref, wao_ref, wmo_ref, o_ref):
    ao = jnp.dot(attn_ref[...], wao_ref[...], preferred_element_type=F32)
    merged = cg_ref[...].astype(F32) + jax.nn.sigmoid(ga_ref[...].astype(F32)) * ao
    o_ref[...] = x_ref[...] + jnp.dot(merged.astype(BF16), wmo_ref[...], preferred_element_type=F32)


def _mix(attn, proj2d, cg, x, w_ao, w_mo, tm, col_ga):
    T, D = x.shape
    A = attn.shape[1]
    tok = lambda c: pl.BlockSpec((tm, c), lambda i: (i, 0))
    return pl.pallas_call(
        _mix_kernel,
        grid=(T // tm,),
        in_specs=[tok(A), pl.BlockSpec((tm, D), lambda i: (i, col_ga)), tok(D), tok(D),
                  pl.BlockSpec((A, D), lambda i: (0, 0)), pl.BlockSpec((D, D), lambda i: (0, 0))],
        out_specs=tok(D),
        out_shape=jax.ShapeDtypeStruct((T, D), F32),
        compiler_params=_params("parallel"),
        name="attn_out_merge_mix",
    )(attn, proj2d, cg, x, w_ao, w_mo)


def _xattn_kernel(h_ref, g_ref, wq_ref, kv_ref, wo_ref, o_ref, oh_ref):
    D = h_ref.shape[-1]
    hd = D // XATTN_HEADS
    h = h_ref[0]
    hn = _rms(h, g_ref[...]).astype(BF16)
    q = (jnp.dot(hn, wq_ref[...], preferred_element_type=F32) * (hd ** -0.5)).astype(BF16)
    for i in range(XATTN_HEADS):
        kh = kv_ref[0, :, i * hd:(i + 1) * hd]
        vh = kv_ref[0, :, D + i * hd:D + (i + 1) * hd]
        s = _dot_nt(q[:, i * hd:(i + 1) * hd], kh)
        p = jnp.exp(s - jnp.max(s, axis=-1, keepdims=True))
        l = jnp.sum(p, axis=-1, keepdims=True)
        oh_ref[:, i * hd:(i + 1) * hd] = (jnp.dot(p.astype(BF16), vh, preferred_element_type=F32) / l).astype(BF16)
    o_ref[0] = h + jnp.dot(oh_ref[...], wo_ref[...], preferred_element_type=F32)


def _xattn(h, g, w_q, kv, w_o, tm):
    B, S, D = h.shape
    M = kv.shape[1]
    return pl.pallas_call(
        _xattn_kernel,
        grid=(B, S // tm),
        in_specs=[pl.BlockSpec((1, tm, D), lambda b, i: (b, i, 0)),
                  pl.BlockSpec((1, D), lambda b, i: (0, 0)),
                  pl.BlockSpec((D, D), lambda b, i: (0, 0)),
                  pl.BlockSpec((1, M, 2 * D), lambda b, i: (b, 0, 0)),
                  pl.BlockSpec((D, D), lambda b, i: (0, 0))],
        out_specs=pl.BlockSpec((1, tm, D), lambda b, i: (b, i, 0)),
        out_shape=jax.ShapeDtypeStruct((B, S, D), F32),
        scratch_shapes=[pltpu.VMEM((tm, D), BF16)],
        compiler_params=_params("parallel", "parallel"),
        name="memory_cross_attention",
    )(h, g.reshape(1, D), w_q, kv, w_o)


def _sort_pairs(n):
    pairs = []
    p = 1
    while p < n:
        k = p
        while k >= 1:
            for j in range(k % p, n - k, 2 * k):
                for i in range(min(k, n - j - k)):
                    if (i + j) // (2 * p) == (i + j + k) // (2 * p):
                        pairs.append((i + j, i + j + k))
            k //= 2
        p *= 2
    return pairs


def _sort_desc(xs):
    xs = list(xs)
    for i, j in _sort_pairs(len(xs)):
        hi, lo = jnp.maximum(xs[i], xs[j]), jnp.minimum(xs[i], xs[j])
        xs[i], xs[j] = hi, lo
    return xs


def _merge_top(a, b):
    n = len(a)
    xs = [jnp.maximum(a[i], b[n - 1 - i]) for i in range(n)]
    d = n // 2
    while d >= 1:
        for i in range(n):
            if (i // d) % 2 == 0:
                hi, lo = jnp.maximum(xs[i], xs[i + d]), jnp.minimum(xs[i], xs[i + d])
                xs[i], xs[i + d] = hi, lo
        d //= 2
    return xs


def _top_sorted(xs, k):
    groups = [_sort_desc(xs[i:i + k]) for i in range(0, len(xs), k)]
    while len(groups) > 1:
        groups = [_merge_top(groups[i], groups[i + 1]) if i + 1 < len(groups) else groups[i]
                  for i in range(0, len(groups), 2)]
    return groups[0]


def _peer_prep_kernel(h_ref, g_ref, wq_ref, wkh_ref, whk_ref, xnt_ref, s1_ref, s2_ref, tau_ref, skh_ref):
    K = PEER_TOPK
    Hp = PEER_HEADS
    NK = PEER_NKEYS
    W = wkh_ref.shape[-1]
    xn = _rms(h_ref[...], g_ref[...]).astype(BF16)
    xnt_ref[...] = _rms(h_ref[...], g_ref[...]).T.astype(BF16)
    q = jnp.dot(xn, wq_ref[...], preferred_element_type=F32).astype(BF16)

    tops = []
    for half in range(2):
        skh_ref[half] = _dot_nt(wkh_ref[half], q[:, half * W:(half + 1) * W]) * LOG2E
        tops.append(_top_sorted([skh_ref[half, k * Hp:(k + 1) * Hp, :] for k in range(NK)], K))
    v1, v2 = tops

    pairs = [(i, j) for i in range(K) for j in range(K) if (i + 1) * (j + 1) <= K]

    def top_sums(first):
        cand = [v2[j] + first[i] for i, j in pairs]
        cand += [jnp.full_like(cand[0], -jnp.inf)] * (-len(cand) % K)
        return _top_sorted(cand, K)

    mx = v1[0] + v2[0]
    c1 = [v - mx for v in v1]
    z = functools.reduce(lambda a, b: a + b, [jnp.exp2(t) for t in top_sums(c1)])
    lz = jnp.log(z) * LOG2E + 1.0
    tau_ref[...] = top_sums([c - lz for c in c1])[K - 1]
    for k in range(NK):
        s1_ref[k * Hp:(k + 1) * Hp, :] = (skh_ref[0, k * Hp:(k + 1) * Hp, :] - mx) - lz

    s2 = _dot_nt(whk_ref[...], q[:, W:2 * W]) * LOG2E
    for h in range(Hp):
        s2_ref[h] = s2[h * NK:(h + 1) * NK, :]


def _peer_prep(h, g, wq, wkh, whk, tm):
    T, D = h.shape
    Hp, NK = PEER_HEADS, PEER_NKEYS
    Wq = wq.shape[1]
    W = wkh.shape[-1]
    const2 = lambda shape: pl.BlockSpec(shape, lambda i: (0,) * len(shape))
    return pl.pallas_call(
        _peer_prep_kernel,
        grid=(T // tm,),
        in_specs=[pl.BlockSpec((tm, D), lambda i: (i, 0)), const2((1, D)), const2((D, Wq)),
                  const2((2, NK * Hp, W)), const2((Hp * NK, W))],
        out_specs=[pl.BlockSpec((D, tm), lambda i: (0, i)),
                   pl.BlockSpec((NK * Hp, tm), lambda i: (0, i)),
                   pl.BlockSpec((Hp, NK, tm), lambda i: (0, 0, i)),
                   pl.BlockSpec((Hp, tm), lambda i: (0, i))],
        out_shape=[jax.ShapeDtypeStruct((D, T), BF16),
                   jax.ShapeDtypeStruct((NK * Hp, T), F32),
                   jax.ShapeDtypeStruct((Hp, NK, T), F32),
                   jax.ShapeDtypeStruct((Hp, T), F32)],
        scratch_shapes=[pltpu.VMEM((2, NK * Hp, tm), F32)],
        compiler_params=_params("parallel"),
        name="peer_retrieval",
    )(h, g.reshape(1, D), wq, wkh, whk)


def _peer_expert_kernel(xnt_ref, u_ref, vt_ref, s1_ref, s2_ref, tau_ref, h_ref, g_ref, o_ref,
                        acc_ref, at0_ref, at1_ref, gt0_ref, gt1_ref, *, te, tm, nj):
    j = pl.program_id(0)
    NK, Hp = PEER_NKEYS, PEER_HEADS

    @pl.when(j == 0)
    def _():
        acc_ref[...] = jnp.zeros(acc_ref.shape, F32)
        for ref in (at0_ref, at1_ref, gt0_ref, gt1_ref):
            ref[...] = jnp.zeros(ref.shape, ref.dtype)

    def gate_block(at_prev, gt_prev, r, c, k0):
        cs = slice(c * LANES, (c + 1) * LANES)
        ks = slice(k0, k0 + GATE_ROWS)
        w = jnp.zeros((GATE_ROWS, LANES), F32)
        for h in range(Hp):
            total = s2_ref[h, ks, cs] + s1_ref[r * Hp + h:r * Hp + h + 1, cs]
            w = w + jnp.where(total >= tau_ref[h:h + 1, cs], jnp.exp2(total), 0.0)
        rows = slice(r * NK + k0, r * NK + k0 + GATE_ROWS)
        a = at_prev[rows, cs]
        gt_prev[rows, cs] = (a * (1.0 + lax.erf(a * math.sqrt(0.5))) * w).astype(BF16)

    def step(at_cur, at_prev, gt_cur, gt_prev):
        W = MXU_WIDTH
        D = u_ref.shape[1]
        pieces = []
        for n in range(tm // W):
            ts = slice(n * W, (n + 1) * W)
            pieces += [("scores", ts, slice(k * D // 2, (k + 1) * D // 2), k) for k in range(2)]
        for n in range(tm // W):
            ts = slice(n * W, (n + 1) * W)
            pieces += [("values", ts, slice(k * W, (k + 1) * W), k) for k in range(te // W)]
        blocks = [(r, c, k0) for r in range(te // NK) for c in range(tm // LANES) for k0 in range(0, NK, GATE_ROWS)]
        per = len(blocks) // len(pieces)
        for i, (kind, ts, kk, k) in enumerate(pieces):
            if kind == "scores":
                part = jnp.dot(u_ref[:, kk], xnt_ref[kk, ts], preferred_element_type=F32)
                if k == 0:
                    at_cur[:, ts] = part
                else:
                    at_cur[:, ts] += part
            else:
                acc_ref[:, ts] += jnp.dot(vt_ref[0, :, kk], gt_cur[kk, ts], preferred_element_type=F32)
            for blk in blocks[i * per:(i + 1) * per]:
                gate_block(at_prev, gt_prev, *blk)

    pl.when(j % 2 == 0)(lambda: step(at0_ref, at1_ref, gt0_ref, gt1_ref))
    pl.when(j % 2 == 1)(lambda: step(at1_ref, at0_ref, gt1_ref, gt0_ref))

    @pl.when(jnp.maximum(j - 2, 0) % nj == nj - 1)
    def _():
        o_ref[...] = _rms(h_ref[...] + acc_ref[...].T, g_ref[...])
        acc_ref[...] = jnp.zeros(acc_ref.shape, F32)


def _peer_experts(xnt, u, v, s1, s2, tau, h, g, tm, te):
    T, D = h.shape
    NE = u.shape[0]
    Hp, NK = PEER_HEADS, PEER_NKEYS
    rows = te // NK * Hp
    nj = NE // te
    steps = (T // tm) * nj
    assert nj > 1
    vt = v.reshape(nj, te, D).transpose(0, 2, 1)

    def tile(s, lag):
        t = jnp.clip(s - lag, 0, steps - 1)
        return t // nj, t % nj

    return pl.pallas_call(
        functools.partial(_peer_expert_kernel, te=te, tm=tm, nj=nj),
        grid=(steps + 2,),
        in_specs=[pl.BlockSpec((D, tm), lambda s: (0, tile(s, 0)[0])),
                  pl.BlockSpec((te, D), lambda s: (tile(s, 0)[1], 0)),
                  pl.BlockSpec((1, D, te), lambda s: (tile(s, 2)[1], 0, 0)),
                  pl.BlockSpec((rows, tm), lambda s: (tile(s, 1)[1], tile(s, 1)[0])),
                  pl.BlockSpec((Hp, NK, tm), lambda s: (0, 0, tile(s, 1)[0])),
                  pl.BlockSpec((Hp, tm), lambda s: (0, tile(s, 1)[0])),
                  pl.BlockSpec((tm, D), lambda s: (tile(s, 2)[0], 0)),
                  pl.BlockSpec((1, D), lambda s: (0, 0))],
        out_specs=pl.BlockSpec((tm, D), lambda s: (tile(s, 2)[0], 0)),
        out_shape=jax.ShapeDtypeStruct((T, D), F32),
        scratch_shapes=[pltpu.VMEM((D, tm), F32), pltpu.VMEM((te, tm), F32), pltpu.VMEM((te, tm), F32),
                        pltpu.VMEM((te, tm), BF16), pltpu.VMEM((te, tm), BF16)],
        compiler_params=_params("arbitrary"),
        name="peer_experts",
    )(xnt, u, vt, s1, s2, tau, h, g.reshape(1, D))


def _layer(h, mem, mix_norm_g, w_in, conv_dw_w, conv_dw_b, conv_ln_g, conv_ln_b, w_conv_out, b_conv_out,
           w_attn_out, w_mix_out, xattn_norm_g, mem_norm_g, w_xq, w_xkv, w_xo, ffn_norm_g, w_peer_q,
           peer_subkeys, peer_u, peer_v, out_norm_g):
    B, S, D = h.shape
    T = B * S
    M = mem.shape[1]
    A = ATTN_HEADS * LANES
    C = conv_dw_w.shape[-1]
    assert A == D and C == D, "column-block addressing of the combined projection assumes equal widths"
    Hp, NK = PEER_HEADS, PEER_NKEYS
    half = peer_subkeys.shape[-1]

    x2 = h.reshape(T, D)
    proj = _rms_matmul(x2, mix_norm_g, w_in.astype(BF16), tm=min(1024, T), tn=w_in.shape[1] // 2)
    proj3 = proj.reshape(B, S, -1)
    cg = _conv_branch(proj3, conv_dw_w, conv_dw_b, conv_ln_g, conv_ln_b, w_conv_out.astype(BF16), b_conv_out,
                      ts=min(512, S), col_a=3, col_b=4, col_g=5)
    slopes = 2.0 ** (-8.0 * jnp.arange(1, ATTN_HEADS + 1, dtype=F32) / ATTN_HEADS)
    attn = _moba(proj3, slopes, S, col_q=0, col_k=ATTN_HEADS, col_v=2 * ATTN_HEADS)
    h1 = _mix(attn.reshape(T, A), proj, cg.reshape(T, D), x2, w_attn_out.astype(BF16), w_mix_out.astype(BF16),
              tm=min(512, T), col_ga=6)

    kv = _rms_matmul(mem.reshape(B * M, D), mem_norm_g, w_xkv.astype(BF16), tm=min(512, B * M), tn=D)
    h2 = _xattn(h1.reshape(B, S, D), xattn_norm_g, w_xq.astype(BF16), kv.reshape(B, M, 2 * D), w_xo.astype(BF16),
                tm=min(512, S)).reshape(T, D)

    wq = w_peer_q.reshape(D, Hp, 2, half).transpose(0, 2, 1, 3).reshape(D, 2 * Hp * half).astype(BF16)
    eye = jnp.eye(Hp, dtype=F32)
    sk = peer_subkeys.astype(F32)
    wkh = jnp.einsum('hpkd,hg->pkhgd', sk, eye).reshape(2, NK * Hp, Hp * half).astype(BF16)
    whk = jnp.einsum('hkd,hg->hkgd', sk[:, 1], eye).reshape(Hp * NK, Hp * half).astype(BF16)
    xnt, s1, s2, tau = _peer_prep(h2, ffn_norm_g, wq, wkh, whk, tm=min(512, T))
    return _peer_experts(xnt, peer_u.astype(BF16), peer_v.astype(BF16), s1, s2, tau, h2, out_norm_g,
                         tm=min(512, T), te=512)


def kernel(x, mem, mix_norm_g, w_in, conv_dw_w, conv_dw_b, conv_ln_g, conv_ln_b, w_conv_out, b_conv_out, w_attn_out, w_mix_out, xattn_norm_g, mem_norm_g, w_xq, w_xkv, w_xo, ffn_norm_g, w_peer_q, peer_subkeys, peer_u, peer_v, final_norm_g):
    depth = w_in.shape[0]
    assert depth == 1, "the last layer's kernel applies the final norm; deeper stacks need a plain-residual variant"
    B, S, D = x.shape
    out = _layer(x, mem, mix_norm_g[0], w_in[0], conv_dw_w[0], conv_dw_b[0], conv_ln_g[0], conv_ln_b[0],
                 w_conv_out[0], b_conv_out[0], w_attn_out[0], w_mix_out[0], xattn_norm_g[0], mem_norm_g[0],
                 w_xq[0], w_xkv[0], w_xo[0], ffn_norm_g[0], w_peer_q[0], peer_subkeys[0], peer_u[0], peer_v[0],
                 final_norm_g)
    return out.reshape(B, S, D)
```

```python
import functools
import math

import jax
import jax.numpy as jnp
from jax import lax
from jax.experimental import pallas as pl
from jax.experimental.pallas import tpu as pltpu

F32 = jnp.float32
BF16 = jnp.bfloat16

EPS = 1e-6
ATTN_HEADS = 8
MOBA_BLOCK = 256
MOBA_TOPK = 3
CONV_WIDTH = 31
XATTN_HEADS = 4
PEER_HEADS = 8
PEER_NKEYS = 128
PEER_TOPK = 16
LOG2E = math.log2(math.e)

LANES = 128
SUBLANES = 8
MXU_WIDTH = 256
GATE_ROWS = 32
VMEM_LIMIT = 56 * 1024 * 1024
CONV_HALO = 32


def _params(*sem):
    return pltpu.CompilerParams(dimension_semantics=sem, vmem_limit_bytes=VMEM_LIMIT)


def _dot_nt(a, b):
    return lax.dot_general(a, b, (((1,), (1,)), ((), ())), preferred_element_type=F32)


def _rms(x, g):
    return x * lax.rsqrt(jnp.mean(x * x, axis=-1, keepdims=True) + EPS) * g


def _rms_matmul_kernel(x_ref, g_ref, w_ref, o_ref, xn_ref):
    @pl.when(pl.program_id(1) == 0)
    def _():
        xn_ref[...] = _rms(x_ref[...], g_ref[...]).astype(BF16)

    o_ref[...] = jnp.dot(xn_ref[...], w_ref[...], preferred_element_type=F32).astype(o_ref.dtype)


def _rms_matmul(x, g, w, tm, tn):
    T, D = x.shape
    N = w.shape[1]
    return pl.pallas_call(
        _rms_matmul_kernel,
        grid=(T // tm, N // tn),
        in_specs=[pl.BlockSpec((tm, D), lambda i, j: (i, 0)),
                  pl.BlockSpec((1, D), lambda i, j: (0, 0)),
                  pl.BlockSpec((D, tn), lambda i, j: (0, j))],
        out_specs=pl.BlockSpec((tm, tn), lambda i, j: (i, j)),
        out_shape=jax.ShapeDtypeStruct((T, N), BF16),
        scratch_shapes=[pltpu.VMEM((tm, D), BF16)],
        compiler_params=_params("parallel", "arbitrary"),
        name="rms_in_proj",
    )(x, g.reshape(1, D), w)


def _conv_kernel(a_ref, b_ref, gc_ref, dww_ref, dwb_ref, lng_ref, lnb_ref, w_ref, bo_ref, o_ref,
                 ph_ref, act_ref, *, ts, rc):
    C = a_ref.shape[-1]
    half = C // 2
    ext = CONV_HALO + ts

    @pl.when(pl.program_id(1) == 0)
    def _():
        ph_ref[0, 0:CONV_HALO, :] = jnp.zeros((CONV_HALO, C), F32)
        ph_ref[0, ext:ext + SUBLANES, :] = jnp.zeros((SUBLANES, C), F32)

    ph_ref[0, CONV_HALO:ext, :] = a_ref[0].astype(F32) * jax.nn.sigmoid(b_ref[0].astype(F32))

    def phase_chunk(i, carry):
        r0 = pl.multiple_of(i * CONV_HALO, CONV_HALO)
        for c0 in (0, half):
            win = ph_ref[0, pl.ds(r0, CONV_HALO + SUBLANES), c0:c0 + half]
            for j in range(1, SUBLANES):
                ph_ref[j, pl.ds(r0, CONV_HALO), c0:c0 + half] = win[j:j + CONV_HALO, :]
        return carry

    lax.fori_loop(0, ext // CONV_HALO, phase_chunk, 0)

    first = CONV_HALO - (CONV_WIDTH - 1)

    def row_chunk(r, carry):
        r0 = pl.multiple_of(r * rc, rc)
        parts = []
        for c0 in (0, half):
            accs = [dwb_ref[:, c0:c0 + half]] * (rc // SUBLANES)
            for w in range(CONV_WIDTH):
                k, j = divmod(first + w, SUBLANES)
                wgt = dww_ref[w, :, c0:c0 + half]
                for g in range(rc // SUBLANES):
                    rows = pl.ds(pl.multiple_of(r0 + (k + g) * SUBLANES, SUBLANES), SUBLANES)
                    accs[g] = accs[g] + wgt * ph_ref[j, rows, c0:c0 + half]
            parts.append(jnp.concatenate(accs, axis=0))
        y = jnp.concatenate(parts, axis=1)
        mu = jnp.mean(y, axis=-1, keepdims=True)
        yc = y - mu
        var = jnp.mean(yc * yc, axis=-1, keepdims=True)
        z = yc * lax.rsqrt(var + EPS) * lng_ref[...] + lnb_ref[...]
        act_ref[pl.ds(r0, rc), :] = (z * jax.nn.sigmoid(z)).astype(BF16)
        return carry

    lax.fori_loop(0, ts // rc, row_chunk, 0)

    ph_ref[0, 0:CONV_HALO, :] = ph_ref[0, ts:ts + CONV_HALO, :]
    co = jnp.dot(act_ref[...], w_ref[...], preferred_element_type=F32) + bo_ref[...]
    o_ref[0] = (jax.nn.sigmoid(gc_ref[0].astype(F32)) * co).astype(o_ref.dtype)


def _conv_branch(proj, dw_w, dw_b, ln_g, ln_b, w_out, b_out, ts, col_a, col_b, col_g):
    B, S, _ = proj.shape
    C = w_out.shape[0]
    D = w_out.shape[1]
    vec = lambda c: pl.BlockSpec((1, c), lambda b, s: (0, 0))
    return pl.pallas_call(
        functools.partial(_conv_kernel, ts=ts, rc=32),
        grid=(B, S // ts),
        in_specs=[pl.BlockSpec((1, ts, C), lambda b, s: (b, s, col_a)),
                  pl.BlockSpec((1, ts, C), lambda b, s: (b, s, col_b)),
                  pl.BlockSpec((1, ts, D), lambda b, s: (b, s, col_g)),
                  pl.BlockSpec((CONV_WIDTH, SUBLANES, C), lambda b, s: (0, 0, 0)),
                  pl.BlockSpec((SUBLANES, C), lambda b, s: (0, 0)),
                  vec(C), vec(C),
                  pl.BlockSpec((C, D), lambda b, s: (0, 0)),
                  vec(D)],
        out_specs=pl.BlockSpec((1, ts, D), lambda b, s: (b, s, 0)),
        out_shape=jax.ShapeDtypeStruct((B, S, D), BF16),
        scratch_shapes=[pltpu.VMEM((SUBLANES, CONV_HALO + ts + SUBLANES, C), F32), pltpu.VMEM((ts, C), BF16)],
        compiler_params=_params("parallel", "arbitrary"),
        name="conv_branch",
    )(proj, proj, proj,
      jnp.broadcast_to(dw_w[:, None, :], (CONV_WIDTH, SUBLANES, C)),
      jnp.broadcast_to(dw_b[None, :], (SUBLANES, C)),
      ln_g.reshape(1, C), ln_b.reshape(1, C), w_out, b_out.reshape(1, D))


MOBA_GROUP = 4
MASK_COLS = 2 * SUBLANES
BIAS_COLS = 3
MASKED = -1e9


def _moba_kernel(q_ref, k_ref, v_ref, ka_ref, o_ref, km_ref, qa_ref, m_ref, l_ref, acc_ref, *, nb, scale):
    L = MOBA_BLOCK
    G = MOBA_GROUP
    hd = LANES
    qi = pl.program_id(2)

    @pl.when(qi == 0)
    def _():
        km_ref[...] = jnp.zeros(km_ref.shape, F32)
        for g in range(G):
            for n in range(nb):
                kb = k_ref[0, n * L:(n + 1) * L, g * hd:(g + 1) * hd].astype(F32)
                km_ref[g, n:n + 1, :] = jnp.mean(kb, axis=0, keepdims=True)

    nrow = lax.broadcasted_iota(jnp.int32, (MASK_COLS, L), 0)
    lane = lax.broadcasted_iota(jnp.int32, (L, LANES), 1)
    for g in range(G):
        q = q_ref[0, :, g * hd:(g + 1) * hd]
        km = km_ref[g]
        km_hi = km.astype(BF16)
        km_lo = (km - km_hi.astype(F32)).astype(BF16)
        gate = _dot_nt(km_hi, q) + _dot_nt(km_lo, q)
        rank = jnp.zeros(gate.shape, F32)
        for m in range(nb - 1):
            gm = gate[m:m + 1, :]
            ahead = (gm > gate) | ((gm == gate) & (nrow > m))
            rank = rank + jnp.where(ahead, 1.0, 0.0) * (qi > m).astype(F32)
        chosen = ((rank < float(MOBA_TOPK)) & (nrow < qi)) | (nrow == qi)
        bias_t = jnp.where(chosen, 0.0, MASKED)
        bias = jnp.concatenate([bias_t, jnp.zeros((LANES - MASK_COLS, L), F32)], axis=0).T
        aug = jnp.where(lane < MASK_COLS, bias, jnp.where(lane < MASK_COLS + BIAS_COLS, 1.0, 0.0))
        qa_ref[g] = jnp.concatenate([(q.astype(F32) * scale).astype(BF16), aug.astype(BF16)], axis=1)

    def scores(g, rows):
        kb = jnp.concatenate([k_ref[0, rows, g * hd:(g + 1) * hd], ka_ref[g, rows, :]], axis=1)
        return _dot_nt(qa_ref[g], kb)

    own = pl.ds(pl.multiple_of(qi * L, L), L)
    row = lax.broadcasted_iota(jnp.int32, (L, L), 0)
    col = lax.broadcasted_iota(jnp.int32, (L, L), 1)
    for g in range(G):
        s = jnp.where(col <= row, scores(g, own), MASKED)
        m0 = jnp.max(s, axis=1, keepdims=True)
        p = jnp.exp(s - m0)
        m_ref[g] = jnp.broadcast_to(m0, (L, hd))
        l_ref[g] = jnp.broadcast_to(jnp.sum(p, axis=1, keepdims=True), (L, hd))
        acc_ref[g] = jnp.dot(p.astype(BF16), v_ref[0, own, g * hd:(g + 1) * hd], preferred_element_type=F32)

    def attend(rows, width):
        for g in range(G):
            s = scores(g, rows)
            m_prev = m_ref[g]
            m_new = jnp.maximum(m_prev, jnp.max(s, axis=1, keepdims=True))
            alpha = jnp.exp(m_prev - m_new)
            ps = [jnp.exp(s[:, c:c + hd] - m_new) for c in range(0, width, hd)]
            l_ref[g] = alpha * l_ref[g] + jnp.sum(functools.reduce(lambda a, b: a + b, ps), axis=1, keepdims=True)
            pv = jnp.dot(jnp.concatenate(ps, axis=1).astype(BF16), v_ref[0, rows, g * hd:(g + 1) * hd],
                         preferred_element_type=F32)
            acc_ref[g] = alpha * acc_ref[g] + pv
            m_ref[g] = m_new

    def past_pair(i, carry):
        attend(pl.ds(pl.multiple_of(i * 2 * L, 2 * L), 2 * L), 2 * L)
        return carry

    lax.fori_loop(0, qi // 2, past_pair, 0)

    @pl.when(qi % 2 == 1)
    def _():
        attend(pl.ds(pl.multiple_of((qi - 1) * L, L), L), L)

    for g in range(G):
        o_ref[0, :, g * hd:(g + 1) * hd] = (acc_ref[g] / l_ref[g]).astype(o_ref.dtype)


def _bf16_part(x):
    bits = lax.bitcast_convert_type(x, jnp.uint32) & jnp.uint32(0xFFFF0000)
    return lax.bitcast_convert_type(bits, F32)


def _moba(proj, slopes, S, col_q, col_k, col_v):
    B = proj.shape[0]
    H, G, L, hd = ATTN_HEADS, MOBA_GROUP, MOBA_BLOCK, LANES
    nb = S // L
    assert nb * L == S and nb <= MASK_COLS and H % G == 0 and col_q % G == col_k % G == col_v % G == 0
    pos = jnp.arange(S, dtype=F32)
    kbias = slopes.astype(F32)[:, None] * pos[None, :]
    hi = _bf16_part(kbias)
    mid = _bf16_part(kbias - hi)
    lo = _bf16_part(kbias - hi - mid)
    lane = jnp.arange(LANES)[None, None, :]
    block = (jnp.arange(S) // L)[None, :, None]
    ka = jnp.where(lane == block, 1.0, 0.0)
    for n, part in enumerate((hi, mid, lo)):
        ka = jnp.where(lane == MASK_COLS + n, part[..., None], ka)
    ka = ka.astype(BF16)
    return pl.pallas_call(
        functools.partial(_moba_kernel, nb=nb, scale=hd ** -0.5),
        grid=(B, H // G, nb),
        in_specs=[pl.BlockSpec((1, L, G * hd), lambda b, h, i: (b, i, col_q // G + h)),
                  pl.BlockSpec((1, S, G * hd), lambda b, h, i: (b, 0, col_k // G + h)),
                  pl.BlockSpec((1, S, G * hd), lambda b, h, i: (b, 0, col_v // G + h)),
                  pl.BlockSpec((G, S, LANES), lambda b, h, i: (h, 0, 0))],
        out_specs=pl.BlockSpec((1, L, G * hd), lambda b, h, i: (b, i, h)),
        out_shape=jax.ShapeDtypeStruct((B, S, H * hd), BF16),
        scratch_shapes=[pltpu.VMEM((G, MASK_COLS, hd), F32),
                        pltpu.VMEM((G, L, 2 * hd), BF16),
                        pltpu.VMEM((G, L, hd), F32), pltpu.VMEM((G, L, hd), F32), pltpu.VMEM((G, L, hd), F32)],
        compiler_params=_params("parallel", "parallel", "arbitrary"),
        name="moba_attention",
    )(proj, proj, proj, ka)


def _mix_kernel(attn_ref, ga_ref, cg_ref, x_ref, wao_ref, wmo_ref, o_ref):
    ao = jnp.dot(attn_ref[...], wao_ref[...], preferred_element_type=F32)
    merged = cg_ref[...].astype(F32) + jax.nn.sigmoid(ga_ref[...].astype(F32)) * ao
    o_ref[...] = x_ref[...] + jnp.dot(merged.astype(BF16), wmo_ref[...], preferred_element_type=F32)


def _mix(attn, proj2d, cg, x, w_ao, w_mo, tm, col_ga):
    T, D = x.shape
    A = attn.shape[1]
    tok = lambda c: pl.BlockSpec((tm, c), lambda i: (i, 0))
    return pl.pallas_call(
        _mix_kernel,
        grid=(T // tm,),
        in_specs=[tok(A), pl.BlockSpec((tm, D), lambda i: (i, col_ga)), tok(D), tok(D),
                  pl.BlockSpec((A, D), lambda i: (0, 0)), pl.BlockSpec((D, D), lambda i: (0, 0))],
        out_specs=tok(D),
        out_shape=jax.ShapeDtypeStruct((T, D), F32),
        compiler_params=_params("parallel"),
        name="attn_out_merge_mix",
    )(attn, proj2d, cg, x, w_ao, w_mo)


def _xattn_kernel(h_ref, g_ref, wq_ref, kv_ref, wo_ref, o_ref, oh_ref):
    D = h_ref.shape[-1]
    hd = D // XATTN_HEADS
    h = h_ref[0]
    hn = _rms(h, g_ref[...]).astype(BF16)
    q = (jnp.dot(hn, wq_ref[...], preferred_element_type=F32) * (hd ** -0.5)).astype(BF16)
    for i in range(XATTN_HEADS):
        kh = kv_ref[0, :, i * hd:(i + 1) * hd]
        vh = kv_ref[0, :, D + i * hd:D + (i + 1) * hd]
        s = _dot_nt(q[:, i * hd:(i + 1) * hd], kh)
        p = jnp.exp(s - jnp.max(s, axis=-1, keepdims=True))
        l = jnp.sum(p, axis=-1, keepdims=True)
        oh_ref[:, i * hd:(i + 1) * hd] = (jnp.dot(p.astype(BF16), vh, preferred_element_type=F32) / l).astype(BF16)
    o_ref[0] = h + jnp.dot(oh_ref[...], wo_ref[...], preferred_element_type=F32)


def _xattn(h, g, w_q, kv, w_o, tm):
    B, S, D = h.shape
    M = kv.shape[1]
    return pl.pallas_call(
        _xattn_kernel,
        grid=(B, S // tm),
        in_specs=[pl.BlockSpec((1, tm, D), lambda b, i: (b, i, 0)),
                  pl.BlockSpec((1, D), lambda b, i: (0, 0)),
                  pl.BlockSpec((D, D), lambda b, i: (0, 0)),
                  pl.BlockSpec((1, M, 2 * D), lambda b, i: (b, 0, 0)),
                  pl.BlockSpec((D, D), lambda b, i: (0, 0))],
        out_specs=pl.BlockSpec((1, tm, D), lambda b, i: (b, i, 0)),
        out_shape=jax.ShapeDtypeStruct((B, S, D), F32),
        scratch_shapes=[pltpu.VMEM((tm, D), BF16)],
        compiler_params=_params("parallel", "parallel"),
        name="memory_cross_attention",
    )(h, g.reshape(1, D), w_q, kv, w_o)


def _sort_pairs(n):
    pairs = []
    p = 1
    while p < n:
        k = p
        while k >= 1:
            for j in range(k % p, n - k, 2 * k):
                for i in range(min(k, n - j - k)):
                    if (i + j) // (2 * p) == (i + j + k) // (2 * p):
                        pairs.append((i + j, i + j + k))
            k //= 2
        p *= 2
    return pairs


def _sort_desc(xs):
    xs = list(xs)
    for i, j in _sort_pairs(len(xs)):
        hi, lo = jnp.maximum(xs[i], xs[j]), jnp.minimum(xs[i], xs[j])
        xs[i], xs[j] = hi, lo
    return xs


def _merge_top(a, b):
    n = len(a)
    xs = [jnp.maximum(a[i], b[n - 1 - i]) for i in range(n)]
    d = n // 2
    while d >= 1:
        for i in range(n):
            if (i // d) % 2 == 0:
                hi, lo = jnp.maximum(xs[i], xs[i + d]), jnp.minimum(xs[i], xs[i + d])
                xs[i], xs[i + d] = hi, lo
        d //= 2
    return xs


def _top_sorted(xs, k):
    groups = [_sort_desc(xs[i:i + k]) for i in range(0, len(xs), k)]
    while len(groups) > 1:
        groups = [_merge_top(groups[i], groups[i + 1]) if i + 1 < len(groups) else groups[i]
                  for i in range(0, len(groups), 2)]
    return groups[0]


def _peer_prep_kernel(h_ref, g_ref, wq_ref, wkh_ref, whk_ref, xnt_ref, s1_ref, s2_ref, tau_ref, skh_ref):
    K = PEER_TOPK
    Hp = PEER_HEADS
    NK = PEER_NKEYS
    W = wkh_ref.shape[-1]
    xn = _rms(h_ref[...], g_ref[...]).astype(BF16)
    xnt_ref[...] = _rms(h_ref[...], g_ref[...]).T.astype(BF16)
    q = jnp.dot(xn, wq_ref[...], preferred_element_type=F32).astype(BF16)

    tops = []
    for half in range(2):
        skh_ref[half] = _dot_nt(wkh_ref[half], q[:, half * W:(half + 1) * W]) * LOG2E
        tops.append(_top_sorted([skh_ref[half, k * Hp:(k + 1) * Hp, :] for k in range(NK)], K))
    v1, v2 = tops

    pairs = [(i, j) for i in range(K) for j in range(K) if (i + 1) * (j + 1) <= K]

    def top_sums(first):
        cand = [v2[j] + first[i] for i, j in pairs]
        cand += [jnp.full_like(cand[0], -jnp.inf)] * (-len(cand) % K)
        return _top_sorted(cand, K)

    mx = v1[0] + v2[0]
    c1 = [v - mx for v in v1]
    z = functools.reduce(lambda a, b: a + b, [jnp.exp2(t) for t in top_sums(c1)])
    lz = jnp.log(z) * LOG2E + 1.0
    tau_ref[...] = top_sums([c - lz for c in c1])[K - 1]
    for k in range(NK):
        s1_ref[k * Hp:(k + 1) * Hp, :] = (skh_ref[0, k * Hp:(k + 1) * Hp, :] - mx) - lz

    s2 = _dot_nt(whk_ref[...], q[:, W:2 * W]) * LOG2E
    for h in range(Hp):
        s2_ref[h] = s2[h * NK:(h + 1) * NK, :]


def _peer_prep(h, g, wq, wkh, whk, tm):
    T, D = h.shape
    Hp, NK = PEER_HEADS, PEER_NKEYS
    Wq = wq.shape[1]
    W = wkh.shape[-1]
    const2 = lambda shape: pl.BlockSpec(shape, lambda i: (0,) * len(shape))
    return pl.pallas_call(
        _peer_prep_kernel,
        grid=(T // tm,),
        in_specs=[pl.BlockSpec((tm, D), lambda i: (i, 0)), const2((1, D)), const2((D, Wq)),
                  const2((2, NK * Hp, W)), const2((Hp * NK, W))],
        out_specs=[pl.BlockSpec((D, tm), lambda i: (0, i)),
                   pl.BlockSpec((NK * Hp, tm), lambda i: (0, i)),
                   pl.BlockSpec((Hp, NK, tm), lambda i: (0, 0, i)),
                   pl.BlockSpec((Hp, tm), lambda i: (0, i))],
        out_shape=[jax.ShapeDtypeStruct((D, T), BF16),
                   jax.ShapeDtypeStruct((NK * Hp, T), F32),
                   jax.ShapeDtypeStruct((Hp, NK, T), F32),
                   jax.ShapeDtypeStruct((Hp, T), F32)],
        scratch_shapes=[pltpu.VMEM((2, NK * Hp, tm), F32)],
        compiler_params=_params("parallel"),
        name="peer_retrieval",
    )(h, g.reshape(1, D), wq, wkh, whk)


def _peer_expert_kernel(xnt_ref, u_ref, vt_ref, s1_ref, s2_ref, tau_ref, h_ref, g_ref, o_ref,
                        acc_ref, at0_ref, at1_ref, gt0_ref, gt1_ref, *, te, tm, nj):
    j = pl.program_id(0)
    NK, Hp = PEER_NKEYS, PEER_HEADS

    @pl.when(j == 0)
    def _():
        acc_ref[...] = jnp.zeros(acc_ref.shape, F32)
        for ref in (at0_ref, at1_ref, gt0_ref, gt1_ref):
            ref[...] = jnp.zeros(ref.shape, ref.dtype)

    def gate_block(at_prev, gt_prev, r, c, k0):
        cs = slice(c * LANES, (c + 1) * LANES)
        ks = slice(k0, k0 + GATE_ROWS)
        w = jnp.zeros((GATE_ROWS, LANES), F32)
        for h in range(Hp):
            total = s2_ref[h, ks, cs] + s1_ref[r * Hp + h:r * Hp + h + 1, cs]
            w = w + jnp.where(total >= tau_ref[h:h + 1, cs], jnp.exp2(total), 0.0)
        rows = slice(r * NK + k0, r * NK + k0 + GATE_ROWS)
        a = at_prev[rows, cs]
        gt_prev[rows, cs] = (a * (1.0 + lax.erf(a * math.sqrt(0.5))) * w).astype(BF16)

    def step(at_cur, at_prev, gt_cur, gt_prev):
        W = MXU_WIDTH
        D = u_ref.shape[1]
        pieces = []
        for n in range(tm // W):
            ts = slice(n * W, (n + 1) * W)
            pieces += [("scores", ts, slice(k * D // 2, (k + 1) * D // 2), k) for k in range(2)]
        for n in range(tm // W):
            ts = slice(n * W, (n + 1) * W)
            pieces += [("values", ts, slice(k * W, (k + 1) * W), k) for k in range(te // W)]
        blocks = [(r, c, k0) for r in range(te // NK) for c in range(tm // LANES) for k0 in range(0, NK, GATE_ROWS)]
        per = len(blocks) // len(pieces)
        for i, (kind, ts, kk, k) in enumerate(pieces):
            if kind == "scores":
                part = jnp.dot(u_ref[:, kk], xnt_ref[kk, ts], preferred_element_type=F32)
                if k == 0:
                    at_cur[:, ts] = part
                else:
                    at_cur[:, ts] += part
            else:
                acc_ref[:, ts] += jnp.dot(vt_ref[0, :, kk], gt_cur[kk, ts], preferred_element_type=F32)
            for blk in blocks[i * per:(i + 1) * per]:
                gate_block(at_prev, gt_prev, *blk)

    pl.when(j % 2 == 0)(lambda: step(at0_ref, at1_ref, gt0_ref, gt1_ref))
    pl.when(j % 2 == 1)(lambda: step(at1_ref, at0_ref, gt1_ref, gt0_ref))

    @pl.when(jnp.maximum(j - 2, 0) % nj == nj - 1)
    def _():
        o_ref[...] = _rms(h_ref[...] + acc_ref[...].T, g_ref[...])
        acc_ref[...] = jnp.zeros(acc_ref.shape, F32)


def _peer_experts(xnt, u, v, s1, s2, tau, h, g, tm, te):
    T, D = h.shape
    NE = u.shape[0]
    Hp, NK = PEER_HEADS, PEER_NKEYS
    rows = te // NK * Hp
    nj = NE // te
    steps = (T // tm) * nj
    assert nj > 1
    vt = v.reshape(nj, te, D).transpose(0, 2, 1)

    def tile(s, lag):
        t = jnp.clip(s - lag, 0, steps - 1)
        return t // nj, t % nj

    return pl.pallas_call(
        functools.partial(_peer_expert_kernel, te=te, tm=tm, nj=nj),
        grid=(steps + 2,),
        in_specs=[pl.BlockSpec((D, tm), lambda s: (0, tile(s, 0)[0])),
                  pl.BlockSpec((te, D), lambda s: (tile(s, 0)[1], 0)),
                  pl.BlockSpec((1, D, te), lambda s: (tile(s, 2)[1], 0, 0)),
                  pl.BlockSpec((rows, tm), lambda s: (tile(s, 1)[1], tile(s, 1)[0])),
                  pl.BlockSpec((Hp, NK, tm), lambda s: (0, 0, tile(s, 1)[0])),
                  pl.BlockSpec((Hp, tm), lambda s: (0, tile(s, 1)[0])),
                  pl.BlockSpec((tm, D), lambda s: (tile(s, 2)[0], 0)),
                  pl.BlockSpec((1, D), lambda s: (0, 0))],
        out_specs=pl.BlockSpec((tm, D), lambda s: (tile(s, 2)[0], 0)),
        out_shape=jax.ShapeDtypeStruct((T, D), F32),
        scratch_shapes=[pltpu.VMEM((D, tm), F32), pltpu.VMEM((te, tm), F32), pltpu.VMEM((te, tm), F32),
                        pltpu.VMEM((te, tm), BF16), pltpu.VMEM((te, tm), BF16)],
        compiler_params=_params("arbitrary"),
        name="peer_experts",
    )(xnt, u, vt, s1, s2, tau, h, g.reshape(1, D))


def _layer(h, mem, mix_norm_g, w_in, conv_dw_w, conv_dw_b, conv_ln_g, conv_ln_b, w_conv_out, b_conv_out,
           w_attn_out, w_mix_out, xattn_norm_g, mem_norm_g, w_xq, w_xkv, w_xo, ffn_norm_g, w_peer_q,
           peer_subkeys, peer_u, peer_v, out_norm_g):
    B, S, D = h.shape
    T = B * S
    M = mem.shape[1]
    A = ATTN_HEADS * LANES
    C = conv_dw_w.shape[-1]
    assert A == D and C == D, "column-block addressing of the combined projection assumes equal widths"
    Hp, NK = PEER_HEADS, PEER_NKEYS
    half = peer_subkeys.shape[-1]

    x2 = h.reshape(T, D)
    proj = _rms_matmul(x2, mix_norm_g, w_in.astype(BF16), tm=min(1024, T), tn=w_in.shape[1] // 2)
    proj3 = proj.reshape(B, S, -1)
    cg = _conv_branch(proj3, conv_dw_w, conv_dw_b, conv_ln_g, conv_ln_b, w_conv_out.astype(BF16), b_conv_out,
                      ts=min(512, S), col_a=3, col_b=4, col_g=5)
    slopes = 2.0 ** (-8.0 * jnp.arange(1, ATTN_HEADS + 1, dtype=F32) / ATTN_HEADS)
    attn = _moba(proj3, slopes, S, col_q=0, col_k=ATTN_HEADS, col_v=2 * ATTN_HEADS)
    h1 = _mix(attn.reshape(T, A), proj, cg.reshape(T, D), x2, w_attn_out.astype(BF16), w_mix_out.astype(BF16),
              tm=min(1024, T), col_ga=6)

    kv = _rms_matmul(mem.reshape(B * M, D), mem_norm_g, w_xkv.astype(BF16), tm=min(512, B * M), tn=D)
    h2 = _xattn(h1.reshape(B, S, D), xattn_norm_g, w_xq.astype(BF16), kv.reshape(B, M, 2 * D), w_xo.astype(BF16),
                tm=min(1024, S)).reshape(T, D)

    wq = w_peer_q.reshape(D, Hp, 2, half).transpose(0, 2, 1, 3).reshape(D, 2 * Hp * half).astype(BF16)
    eye = jnp.eye(Hp, dtype=F32)
    sk = peer_subkeys.astype(F32)
    wkh = jnp.einsum('hpkd,hg->pkhgd', sk, eye).reshape(2, NK * Hp, Hp * half).astype(BF16)
    whk = jnp.einsum('hkd,hg->hkgd', sk[:, 1], eye).reshape(Hp * NK, Hp * half).astype(BF16)
    xnt, s1, s2, tau = _peer_prep(h2, ffn_norm_g, wq, wkh, whk, tm=min(512, T))
    return _peer_experts(xnt, peer_u.astype(BF16), peer_v.astype(BF16), s1, s2, tau, h2, out_norm_g,
                         tm=min(512, T), te=512)


def kernel(x, mem, mix_norm_g, w_in, conv_dw_w, conv_dw_b, conv_ln_g, conv_ln_b, w_conv_out, b_conv_out, w_attn_out, w_mix_out, xattn_norm_g, mem_norm_g, w_xq, w_xkv, w_xo, ffn_norm_g, w_peer_q, peer_subkeys, peer_u, peer_v, final_norm_g):
    depth = w_in.shape[0]
    assert depth == 1, "the last layer's kernel applies the final norm; deeper stacks need a plain-residual variant"
    B, S, D = x.shape
    out = _layer(x, mem, mix_norm_g[0], w_in[0], conv_dw_w[0], conv_dw_b[0], conv_ln_g[0], conv_ln_b[0],
                 w_conv_out[0], b_conv_out[0], w_attn_out[0], w_mix_out[0], xattn_norm_g[0], mem_norm_g[0],
                 w_xq[0], w_xkv[0], w_xo[0], ffn_norm_g[0], w_peer_q[0], peer_subkeys[0], peer_u[0], peer_v[0],
                 final_norm_g)
    return out.reshape(B, S, D)
```

```python
import functools
import math

import jax
import jax.numpy as jnp
from jax import lax
from jax.experimental import pallas as pl
from jax.experimental.pallas import tpu as pltpu

F32 = jnp.float32
BF16 = jnp.bfloat16

EPS = 1e-6
ATTN_HEADS = 8
MOBA_BLOCK = 256
MOBA_TOPK = 3
CONV_WIDTH = 31
XATTN_HEADS = 4
PEER_HEADS = 8
PEER_NKEYS = 128
PEER_TOPK = 16
LOG2E = math.log2(math.e)

LANES = 128
SUBLANES = 8
MXU_WIDTH = 256
GATE_ROWS = 32
VMEM_LIMIT = 56 * 1024 * 1024
CONV_HALO = 32


def _params(*sem):
    return pltpu.CompilerParams(dimension_semantics=sem, vmem_limit_bytes=VMEM_LIMIT)


def _dot_nt(a, b):
    return lax.dot_general(a, b, (((1,), (1,)), ((), ())), preferred_element_type=F32)


def _rms(x, g):
    return x * lax.rsqrt(jnp.mean(x * x, axis=-1, keepdims=True) + EPS) * g


def _rms_matmul_kernel(x_ref, g_ref, w_ref, o_ref, xn_ref):
    @pl.when(pl.program_id(1) == 0)
    def _():
        xn_ref[...] = _rms(x_ref[...], g_ref[...]).astype(BF16)

    o_ref[...] = jnp.dot(xn_ref[...], w_ref[...], preferred_element_type=F32).astype(o_ref.dtype)


def _rms_matmul(x, g, w, tm, tn):
    T, D = x.shape
    N = w.shape[1]
    return pl.pallas_call(
        _rms_matmul_kernel,
        grid=(T // tm, N // tn),
        in_specs=[pl.BlockSpec((tm, D), lambda i, j: (i, 0)),
                  pl.BlockSpec((1, D), lambda i, j: (0, 0)),
                  pl.BlockSpec((D, tn), lambda i, j: (0, j))],
        out_specs=pl.BlockSpec((tm, tn), lambda i, j: (i, j)),
        out_shape=jax.ShapeDtypeStruct((T, N), BF16),
        scratch_shapes=[pltpu.VMEM((tm, D), BF16)],
        compiler_params=_params("parallel", "arbitrary"),
        name="rms_in_proj",
    )(x, g.reshape(1, D), w)


def _conv_kernel(a_ref, b_ref, gc_ref, dww_ref, dwb_ref, lng_ref, lnb_ref, w_ref, bo_ref, o_ref,
                 ph_ref, act_ref, *, ts, rc):
    C = a_ref.shape[-1]
    half = C // 2
    ext = CONV_HALO + ts

    @pl.when(pl.program_id(1) == 0)
    def _():
        ph_ref[0, 0:CONV_HALO, :] = jnp.zeros((CONV_HALO, C), F32)
        ph_ref[0, ext:ext + SUBLANES, :] = jnp.zeros((SUBLANES, C), F32)

    ph_ref[0, CONV_HALO:ext, :] = a_ref[0].astype(F32) * jax.nn.sigmoid(b_ref[0].astype(F32))

    def phase_chunk(i, carry):
        r0 = pl.multiple_of(i * CONV_HALO, CONV_HALO)
        for c0 in (0, half):
            win = ph_ref[0, pl.ds(r0, CONV_HALO + SUBLANES), c0:c0 + half]
            for j in range(1, SUBLANES):
                ph_ref[j, pl.ds(r0, CONV_HALO), c0:c0 + half] = win[j:j + CONV_HALO, :]
        return carry

    lax.fori_loop(0, ext // CONV_HALO, phase_chunk, 0)

    first = CONV_HALO - (CONV_WIDTH - 1)

    def row_chunk(r, carry):
        r0 = pl.multiple_of(r * rc, rc)
        parts = []
        for c0 in (0, half):
            accs = [dwb_ref[:, c0:c0 + half]] * (rc // SUBLANES)
            for w in range(CONV_WIDTH):
                k, j = divmod(first + w, SUBLANES)
                wgt = dww_ref[w, :, c0:c0 + half]
                for g in range(rc // SUBLANES):
                    rows = pl.ds(pl.multiple_of(r0 + (k + g) * SUBLANES, SUBLANES), SUBLANES)
                    accs[g] = accs[g] + wgt * ph_ref[j, rows, c0:c0 + half]
            parts.append(jnp.concatenate(accs, axis=0))
        y = jnp.concatenate(parts, axis=1)
        mu = jnp.mean(y, axis=-1, keepdims=True)
        yc = y - mu
        var = jnp.mean(yc * yc, axis=-1, keepdims=True)
        z = yc * lax.rsqrt(var + EPS) * lng_ref[...] + lnb_ref[...]
        act_ref[pl.ds(r0, rc), :] = (z * jax.nn.sigmoid(z)).astype(BF16)
        return carry

    lax.fori_loop(0, ts // rc, row_chunk, 0)

    ph_ref[0, 0:CONV_HALO, :] = ph_ref[0, ts:ts + CONV_HALO, :]
    co = jnp.dot(act_ref[...], w_ref[...], preferred_element_type=F32) + bo_ref[...]
    o_ref[0] = (jax.nn.sigmoid(gc_ref[0].astype(F32)) * co).astype(o_ref.dtype)


def _conv_branch(proj, dw_w, dw_b, ln_g, ln_b, w_out, b_out, ts, col_a, col_b, col_g):
    B, S, _ = proj.shape
    C = w_out.shape[0]
    D = w_out.shape[1]
    vec = lambda c: pl.BlockSpec((1, c), lambda b, s: (0, 0))
    return pl.pallas_call(
        functools.partial(_conv_kernel, ts=ts, rc=32),
        grid=(B, S // ts),
        in_specs=[pl.BlockSpec((1, ts, C), lambda b, s: (b, s, col_a)),
                  pl.BlockSpec((1, ts, C), lambda b, s: (b, s, col_b)),
                  pl.BlockSpec((1, ts, D), lambda b, s: (b, s, col_g)),
                  pl.BlockSpec((CONV_WIDTH, SUBLANES, C), lambda b, s: (0, 0, 0)),
                  pl.BlockSpec((SUBLANES, C), lambda b, s: (0, 0)),
                  vec(C), vec(C),
                  pl.BlockSpec((C, D), lambda b, s: (0, 0)),
                  vec(D)],
        out_specs=pl.BlockSpec((1, ts, D), lambda b, s: (b, s, 0)),
        out_shape=jax.ShapeDtypeStruct((B, S, D), BF16),
        scratch_shapes=[pltpu.VMEM((SUBLANES, CONV_HALO + ts + SUBLANES, C), F32), pltpu.VMEM((ts, C), BF16)],
        compiler_params=_params("parallel", "arbitrary"),
        name="conv_branch",
    )(proj, proj, proj,
      jnp.broadcast_to(dw_w[:, None, :], (CONV_WIDTH, SUBLANES, C)),
      jnp.broadcast_to(dw_b[None, :], (SUBLANES, C)),
      ln_g.reshape(1, C), ln_b.reshape(1, C), w_out, b_out.reshape(1, D))


MOBA_GROUP = 4
MASK_COLS = 2 * SUBLANES
BIAS_COLS = 3
MASKED = -1e9


def _moba_kernel(q_ref, k_ref, v_ref, ka_ref, o_ref, km_ref, qa_ref, m_ref, l_ref, acc_ref, *, nb, scale):
    L = MOBA_BLOCK
    G = MOBA_GROUP
    hd = LANES
    qi = pl.program_id(2)

    @pl.when(qi == 0)
    def _():
        km_ref[...] = jnp.zeros(km_ref.shape, F32)
        for g in range(G):
            for n in range(nb):
                kb = k_ref[0, n * L:(n + 1) * L, g * hd:(g + 1) * hd].astype(F32)
                km_ref[g, n:n + 1, :] = jnp.mean(kb, axis=0, keepdims=True)

    nrow = lax.broadcasted_iota(jnp.int32, (MASK_COLS, L), 0)
    lane = lax.broadcasted_iota(jnp.int32, (L, LANES), 1)
    for g in range(G):
        q = q_ref[0, :, g * hd:(g + 1) * hd]
        km = km_ref[g]
        km_hi = km.astype(BF16)
        km_lo = (km - km_hi.astype(F32)).astype(BF16)
        gate = _dot_nt(km_hi, q) + _dot_nt(km_lo, q)
        rank = jnp.zeros(gate.shape, F32)
        for m in range(nb - 1):
            gm = gate[m:m + 1, :]
            ahead = (gm > gate) | ((gm == gate) & (nrow > m))
            rank = rank + jnp.where(ahead, 1.0, 0.0) * (qi > m).astype(F32)
        chosen = ((rank < float(MOBA_TOPK)) & (nrow < qi)) | (nrow == qi)
        bias_t = jnp.where(chosen, 0.0, MASKED)
        bias = jnp.concatenate([bias_t, jnp.zeros((LANES - MASK_COLS, L), F32)], axis=0).T
        aug = jnp.where(lane < MASK_COLS, bias, jnp.where(lane < MASK_COLS + BIAS_COLS, 1.0, 0.0))
        qa_ref[g] = jnp.concatenate([(q.astype(F32) * scale).astype(BF16), aug.astype(BF16)], axis=1)

    def scores(g, rows):
        kb = jnp.concatenate([k_ref[0, rows, g * hd:(g + 1) * hd], ka_ref[g, rows, :]], axis=1)
        return _dot_nt(qa_ref[g], kb)

    own = pl.ds(pl.multiple_of(qi * L, L), L)
    row = lax.broadcasted_iota(jnp.int32, (L, L), 0)
    col = lax.broadcasted_iota(jnp.int32, (L, L), 1)
    for g in range(G):
        s = jnp.where(col <= row, scores(g, own), MASKED)
        m0 = jnp.max(s, axis=1, keepdims=True)
        p = jnp.exp(s - m0)
        m_ref[g] = jnp.broadcast_to(m0, (L, hd))
        l_ref[g] = jnp.broadcast_to(jnp.sum(p, axis=1, keepdims=True), (L, hd))
        acc_ref[g] = jnp.dot(p.astype(BF16), v_ref[0, own, g * hd:(g + 1) * hd], preferred_element_type=F32)

    def attend(rows, width):
        for g in range(G):
            s = scores(g, rows)
            m_prev = m_ref[g]
            m_new = jnp.maximum(m_prev, jnp.max(s, axis=1, keepdims=True))
            alpha = jnp.exp(m_prev - m_new)
            ps = [jnp.exp(s[:, c:c + hd] - m_new) for c in range(0, width, hd)]
            l_ref[g] = alpha * l_ref[g] + jnp.sum(functools.reduce(lambda a, b: a + b, ps), axis=1, keepdims=True)
            pv = jnp.dot(jnp.concatenate(ps, axis=1).astype(BF16), v_ref[0, rows, g * hd:(g + 1) * hd],
                         preferred_element_type=F32)
            acc_ref[g] = alpha * acc_ref[g] + pv
            m_ref[g] = m_new

    def past_pair(i, carry):
        attend(pl.ds(pl.multiple_of(i * 2 * L, 2 * L), 2 * L), 2 * L)
        return carry

    lax.fori_loop(0, qi // 2, past_pair, 0)

    @pl.when(qi % 2 == 1)
    def _():
        attend(pl.ds(pl.multiple_of((qi - 1) * L, L), L), L)

    for g in range(G):
        o_ref[0, :, g * hd:(g + 1) * hd] = (acc_ref[g] / l_ref[g]).astype(o_ref.dtype)


def _bf16_part(x):
    bits = lax.bitcast_convert_type(x, jnp.uint32) & jnp.uint32(0xFFFF0000)
    return lax.bitcast_convert_type(bits, F32)


def _moba(proj, slopes, S, col_q, col_k, col_v):
    B = proj.shape[0]
    H, G, L, hd = ATTN_HEADS, MOBA_GROUP, MOBA_BLOCK, LANES
    nb = S // L
    assert nb * L == S and nb <= MASK_COLS and H % G == 0 and col_q % G == col_k % G == col_v % G == 0
    pos = jnp.arange(S, dtype=F32)
    kbias = slopes.astype(F32)[:, None] * pos[None, :]
    hi = _bf16_part(kbias)
    mid = _bf16_part(kbias - hi)
    lo = _bf16_part(kbias - hi - mid)
    lane = jnp.arange(LANES)[None, None, :]
    block = (jnp.arange(S) // L)[None, :, None]
    ka = jnp.where(lane == block, 1.0, 0.0)
    for n, part in enumerate((hi, mid, lo)):
        ka = jnp.where(lane == MASK_COLS + n, part[..., None], ka)
    ka = ka.astype(BF16)
    return pl.pallas_call(
        functools.partial(_moba_kernel, nb=nb, scale=hd ** -0.5),
        grid=(B, H // G, nb),
        in_specs=[pl.BlockSpec((1, L, G * hd), lambda b, h, i: (b, i, col_q // G + h)),
                  pl.BlockSpec((1, S, G * hd), lambda b, h, i: (b, 0, col_k // G + h)),
                  pl.BlockSpec((1, S, G * hd), lambda b, h, i: (b, 0, col_v // G + h)),
                  pl.BlockSpec((G, S, LANES), lambda b, h, i: (h, 0, 0))],
        out_specs=pl.BlockSpec((1, L, G * hd), lambda b, h, i: (b, i, h)),
        out_shape=jax.ShapeDtypeStruct((B, S, H * hd), BF16),
        scratch_shapes=[pltpu.VMEM((G, MASK_COLS, hd), F32),
                        pltpu.VMEM((G, L, 2 * hd), BF16),
                        pltpu.VMEM((G, L, hd), F32), pltpu.VMEM((G, L, hd), F32), pltpu.VMEM((G, L, hd), F32)],
        compiler_params=_params("parallel", "parallel", "arbitrary"),
        name="moba_attention",
    )(proj, proj, proj, ka)


def _mix_kernel(attn_ref, ga_ref, cg_ref, x_ref, wao_ref, wmo_ref, o_ref):
    ao = jnp.dot(attn_ref[...], wao_ref[...], preferred_element_type=F32)
    merged = cg_ref[...].astype(F32) + jax.nn.sigmoid(ga_ref[...].astype(F32)) * ao
    o_ref[...] = x_ref[...] + jnp.dot(merged.astype(BF16), wmo_ref[...], preferred_element_type=F32)


def _mix(attn, proj2d, cg, x, w_ao, w_mo, tm, col_ga):
    T, D = x.shape
    A = attn.shape[1]
    tok = lambda c: pl.BlockSpec((tm, c), lambda i: (i, 0))
    return pl.pallas_call(
        _mix_kernel,
        grid=(T // tm,),
        in_specs=[tok(A), pl.BlockSpec((tm, D), lambda i: (i, col_ga)), tok(D), tok(D),
                  pl.BlockSpec((A, D), lambda i: (0, 0)), pl.BlockSpec((D, D), lambda i: (0, 0))],
        out_specs=tok(D),
        out_shape=jax.ShapeDtypeStruct((T, D), F32),
        compiler_params=_params("parallel"),
        name="attn_out_merge_mix",
    )(attn, proj2d, cg, x, w_ao, w_mo)


def _xattn_kernel(h_ref, g_ref, wq_ref, kv_ref, wo_ref, o_ref, oh_ref):
    D = h_ref.shape[-1]
    hd = D // XATTN_HEADS
    h = h_ref[0]
    hn = _rms(h, g_ref[...]).astype(BF16)
    q = (jnp.dot(hn, wq_ref[...], preferred_element_type=F32) * (hd ** -0.5)).astype(BF16)
    for i in range(XATTN_HEADS):
        kh = kv_ref[0, :, i * hd:(i + 1) * hd]
        vh = kv_ref[0, :, D + i * hd:D + (i + 1) * hd]
        s = _dot_nt(q[:, i * hd:(i + 1) * hd], kh)
        p = jnp.exp(s - jnp.max(s, axis=-1, keepdims=True))
        l = jnp.sum(p, axis=-1, keepdims=True)
        oh_ref[:, i * hd:(i + 1) * hd] = (jnp.dot(p.astype(BF16), vh, preferred_element_type=F32) / l).astype(BF16)
    o_ref[0] = h + jnp.dot(oh_ref[...], wo_ref[...], preferred_element_type=F32)


def _xattn(h, g, w_q, kv, w_o, tm):
    B, S, D = h.shape
    M = kv.shape[1]
    return pl.pallas_call(
        _xattn_kernel,
        grid=(B, S // tm),
        in_specs=[pl.BlockSpec((1, tm, D), lambda b, i: (b, i, 0)),
                  pl.BlockSpec((1, D), lambda b, i: (0, 0)),
                  pl.BlockSpec((D, D), lambda b, i: (0, 0)),
                  pl.BlockSpec((1, M, 2 * D), lambda b, i: (b, 0, 0)),
                  pl.BlockSpec((D, D), lambda b, i: (0, 0))],
        out_specs=pl.BlockSpec((1, tm, D), lambda b, i: (b, i, 0)),
        out_shape=jax.ShapeDtypeStruct((B, S, D), F32),
        scratch_shapes=[pltpu.VMEM((tm, D), BF16)],
        compiler_params=_params("parallel", "parallel"),
        name="memory_cross_attention",
    )(h, g.reshape(1, D), w_q, kv, w_o)


def _sort_pairs(n):
    pairs = []
    p = 1
    while p < n:
        k = p
        while k >= 1:
            for j in range(k % p, n - k, 2 * k):
                for i in range(min(k, n - j - k)):
                    if (i + j) // (2 * p) == (i + j + k) // (2 * p):
                        pairs.append((i + j, i + j + k))
            k //= 2
        p *= 2
    return pairs


def _sort_desc(xs):
    xs = list(xs)
    for i, j in _sort_pairs(len(xs)):
        hi, lo = jnp.maximum(xs[i], xs[j]), jnp.minimum(xs[i], xs[j])
        xs[i], xs[j] = hi, lo
    return xs


def _merge_top(a, b):
    n = len(a)
    xs = [jnp.maximum(a[i], b[n - 1 - i]) for i in range(n)]
    d = n // 2
    while d >= 1:
        for i in range(n):
            if (i // d) % 2 == 0:
                hi, lo = jnp.maximum(xs[i], xs[i + d]), jnp.minimum(xs[i], xs[i + d])
                xs[i], xs[i + d] = hi, lo
        d //= 2
    return xs


def _top_sorted(xs, k):
    groups = [_sort_desc(xs[i:i + k]) for i in range(0, len(xs), k)]
    while len(groups) > 1:
        groups = [_merge_top(groups[i], groups[i + 1]) if i + 1 < len(groups) else groups[i]
                  for i in range(0, len(groups), 2)]
    return groups[0]


def _peer_prep_kernel(h_ref, g_ref, wq_ref, wkh_ref, whk_ref, xnt_ref, s1_ref, s2_ref, tau_ref, skh_ref):
    K = PEER_TOPK
    Hp = PEER_HEADS
    NK = PEER_NKEYS
    W = wkh_ref.shape[-1]
    xn = _rms(h_ref[...], g_ref[...]).astype(BF16)
    xnt_ref[...] = _rms(h_ref[...], g_ref[...]).T.astype(BF16)
    q = jnp.dot(xn, wq_ref[...], preferred_element_type=F32).astype(BF16)

    tops = []
    for half in range(2):
        skh_ref[half] = _dot_nt(wkh_ref[half], q[:, half * W:(half + 1) * W]) * LOG2E
        tops.append(_top_sorted([skh_ref[half, k * Hp:(k + 1) * Hp, :] for k in range(NK)], K))
    v1, v2 = tops

    pairs = [(i, j) for i in range(K) for j in range(K) if (i + 1) * (j + 1) <= K]

    def top_sums(first):
        cand = [v2[j] + first[i] for i, j in pairs]
        cand += [jnp.full_like(cand[0], -jnp.inf)] * (-len(cand) % K)
        return _top_sorted(cand, K)

    mx = v1[0] + v2[0]
    c1 = [v - mx for v in v1]
    z = functools.reduce(lambda a, b: a + b, [jnp.exp2(t) for t in top_sums(c1)])
    lz = jnp.log(z) * LOG2E + 1.0
    tau_ref[...] = top_sums([c - lz for c in c1])[K - 1]
    for k in range(NK):
        s1_ref[k * Hp:(k + 1) * Hp, :] = (skh_ref[0, k * Hp:(k + 1) * Hp, :] - mx) - lz

    s2 = _dot_nt(whk_ref[...], q[:, W:2 * W]) * LOG2E
    for h in range(Hp):
        s2_ref[h] = s2[h * NK:(h + 1) * NK, :]


def _peer_prep(h, g, wq, wkh, whk, tm):
    T, D = h.shape
    Hp, NK = PEER_HEADS, PEER_NKEYS
    Wq = wq.shape[1]
    W = wkh.shape[-1]
    const2 = lambda shape: pl.BlockSpec(shape, lambda i: (0,) * len(shape))
    return pl.pallas_call(
        _peer_prep_kernel,
        grid=(T // tm,),
        in_specs=[pl.BlockSpec((tm, D), lambda i: (i, 0)), const2((1, D)), const2((D, Wq)),
                  const2((2, NK * Hp, W)), const2((Hp * NK, W))],
        out_specs=[pl.BlockSpec((D, tm), lambda i: (0, i)),
                   pl.BlockSpec((NK * Hp, tm), lambda i: (0, i)),
                   pl.BlockSpec((Hp, NK, tm), lambda i: (0, 0, i)),
                   pl.BlockSpec((Hp, tm), lambda i: (0, i))],
        out_shape=[jax.ShapeDtypeStruct((D, T), BF16),
                   jax.ShapeDtypeStruct((NK * Hp, T), F32),
                   jax.ShapeDtypeStruct((Hp, NK, T), F32),
                   jax.ShapeDtypeStruct((Hp, T), F32)],
        scratch_shapes=[pltpu.VMEM((2, NK * Hp, tm), F32)],
        compiler_params=_params("parallel"),
        name="peer_retrieval",
    )(h, g.reshape(1, D), wq, wkh, whk)


def _peer_expert_kernel(xnt_ref, u_ref, vt_ref, s1_ref, s2_ref, tau_ref, h_ref, g_ref, o_ref,
                        acc_ref, at0_ref, at1_ref, gt0_ref, gt1_ref, *, te, tm, nj):
    j = pl.program_id(0)
    NK, Hp = PEER_NKEYS, PEER_HEADS

    @pl.when(j == 0)
    def _():
        acc_ref[...] = jnp.zeros(acc_ref.shape, F32)
        for ref in (at0_ref, at1_ref, gt0_ref, gt1_ref):
            ref[...] = jnp.zeros(ref.shape, ref.dtype)

    def gate_block(at_prev, gt_prev, r, c, k0):
        cs = slice(c * LANES, (c + 1) * LANES)
        ks = slice(k0, k0 + GATE_ROWS)
        w = jnp.zeros((GATE_ROWS, LANES), F32)
        for h in range(Hp):
            total = s2_ref[h, ks, cs] + s1_ref[r * Hp + h:r * Hp + h + 1, cs]
            w = w + jnp.where(total >= tau_ref[h:h + 1, cs], jnp.exp2(total), 0.0)
        rows = slice(r * NK + k0, r * NK + k0 + GATE_ROWS)
        a = at_prev[rows, cs]
        gt_prev[rows, cs] = (a * (1.0 + lax.erf(a * math.sqrt(0.5))) * w).astype(BF16)

    def step(at_cur, at_prev, gt_cur, gt_prev):
        W = MXU_WIDTH
        D = u_ref.shape[1]
        pieces = []
        for n in range(tm // W):
            ts = slice(n * W, (n + 1) * W)
            pieces += [("scores", ts, slice(k * W, (k + 1) * W), k, 1) for k in range(D // W)]
        for n in range(tm // W):
            ts = slice(n * W, (n + 1) * W)
            pieces += [("values", ts, slice(k * W, (k + 1) * W), k, D // te) for k in range(te // W)]
        blocks = [(r, c, k0) for r in range(te // NK) for c in range(tm // LANES) for k0 in range(0, NK, GATE_ROWS)]
        per_cost = len(blocks) // sum(p[4] for p in pieces)
        done = 0
        for kind, ts, kk, k, cost in pieces:
            if kind == "scores":
                part = jnp.dot(u_ref[:, kk], xnt_ref[kk, ts], preferred_element_type=F32)
                if k == 0:
                    at_cur[:, ts] = part
                else:
                    at_cur[:, ts] += part
            else:
                acc_ref[:, ts] += jnp.dot(vt_ref[0, :, kk], gt_cur[kk, ts], preferred_element_type=F32)
            for blk in blocks[done:done + cost * per_cost]:
                gate_block(at_prev, gt_prev, *blk)
            done += cost * per_cost
        assert done == len(blocks)

    pl.when(j % 2 == 0)(lambda: step(at0_ref, at1_ref, gt0_ref, gt1_ref))
    pl.when(j % 2 == 1)(lambda: step(at1_ref, at0_ref, gt1_ref, gt0_ref))

    @pl.when(jnp.maximum(j - 2, 0) % nj == nj - 1)
    def _():
        o_ref[...] = _rms(h_ref[...] + acc_ref[...].T, g_ref[...])
        acc_ref[...] = jnp.zeros(acc_ref.shape, F32)


def _peer_experts(xnt, u, v, s1, s2, tau, h, g, tm, te):
    T, D = h.shape
    NE = u.shape[0]
    Hp, NK = PEER_HEADS, PEER_NKEYS
    rows = te // NK * Hp
    nj = NE // te
    steps = (T // tm) * nj
    assert nj > 1
    vt = v.reshape(nj, te, D).transpose(0, 2, 1)

    def tile(s, lag):
        t = jnp.clip(s - lag, 0, steps - 1)
        return t // nj, t % nj

    return pl.pallas_call(
        functools.partial(_peer_expert_kernel, te=te, tm=tm, nj=nj),
        grid=(steps + 2,),
        in_specs=[pl.BlockSpec((D, tm), lambda s: (0, tile(s, 0)[0])),
                  pl.BlockSpec((te, D), lambda s: (tile(s, 0)[1], 0)),
                  pl.BlockSpec((1, D, te), lambda s: (tile(s, 2)[1], 0, 0)),
                  pl.BlockSpec((rows, tm), lambda s: (tile(s, 1)[1], tile(s, 1)[0])),
                  pl.BlockSpec((Hp, NK, tm), lambda s: (0, 0, tile(s, 1)[0])),
                  pl.BlockSpec((Hp, tm), lambda s: (0, tile(s, 1)[0])),
                  pl.BlockSpec((tm, D), lambda s: (tile(s, 2)[0], 0)),
                  pl.BlockSpec((1, D), lambda s: (0, 0))],
        out_specs=pl.BlockSpec((tm, D), lambda s: (tile(s, 2)[0], 0)),
        out_shape=jax.ShapeDtypeStruct((T, D), F32),
        scratch_shapes=[pltpu.VMEM((D, tm), F32), pltpu.VMEM((te, tm), F32), pltpu.VMEM((te, tm), F32),
                        pltpu.VMEM((te, tm), BF16), pltpu.VMEM((te, tm), BF16)],
        compiler_params=_params("arbitrary"),
        name="peer_experts",
    )(xnt, u, vt, s1, s2, tau, h, g.reshape(1, D))


def _layer(h, mem, mix_norm_g, w_in, conv_dw_w, conv_dw_b, conv_ln_g, conv_ln_b, w_conv_out, b_conv_out,
           w_attn_out, w_mix_out, xattn_norm_g, mem_norm_g, w_xq, w_xkv, w_xo, ffn_norm_g, w_peer_q,
           peer_subkeys, peer_u, peer_v, out_norm_g):
    B, S, D = h.shape
    T = B * S
    M = mem.shape[1]
    A = ATTN_HEADS * LANES
    C = conv_dw_w.shape[-1]
    assert A == D and C == D, "column-block addressing of the combined projection assumes equal widths"
    Hp, NK = PEER_HEADS, PEER_NKEYS
    half = peer_subkeys.shape[-1]

    x2 = h.reshape(T, D)
    proj = _rms_matmul(x2, mix_norm_g, w_in.astype(BF16), tm=min(1024, T), tn=w_in.shape[1] // 2)
    proj3 = proj.reshape(B, S, -1)
    cg = _conv_branch(proj3, conv_dw_w, conv_dw_b, conv_ln_g, conv_ln_b, w_conv_out.astype(BF16), b_conv_out,
                      ts=min(512, S), col_a=3, col_b=4, col_g=5)
    slopes = 2.0 ** (-8.0 * jnp.arange(1, ATTN_HEADS + 1, dtype=F32) / ATTN_HEADS)
    attn = _moba(proj3, slopes, S, col_q=0, col_k=ATTN_HEADS, col_v=2 * ATTN_HEADS)
    h1 = _mix(attn.reshape(T, A), proj, cg.reshape(T, D), x2, w_attn_out.astype(BF16), w_mix_out.astype(BF16),
              tm=min(1024, T), col_ga=6)

    kv = _rms_matmul(mem.reshape(B * M, D), mem_norm_g, w_xkv.astype(BF16), tm=min(512, B * M), tn=D)
    h2 = _xattn(h1.reshape(B, S, D), xattn_norm_g, w_xq.astype(BF16), kv.reshape(B, M, 2 * D), w_xo.astype(BF16),
                tm=min(1024, S)).reshape(T, D)

    wq = w_peer_q.reshape(D, Hp, 2, half).transpose(0, 2, 1, 3).reshape(D, 2 * Hp * half).astype(BF16)
    eye = jnp.eye(Hp, dtype=F32)
    sk = peer_subkeys.astype(F32)
    wkh = jnp.einsum('hpkd,hg->pkhgd', sk, eye).reshape(2, NK * Hp, Hp * half).astype(BF16)
    whk = jnp.einsum('hkd,hg->hkgd', sk[:, 1], eye).reshape(Hp * NK, Hp * half).astype(BF16)
    xnt, s1, s2, tau = _peer_prep(h2, ffn_norm_g, wq, wkh, whk, tm=min(512, T))
    return _peer_experts(xnt, peer_u.astype(BF16), peer_v.astype(BF16), s1, s2, tau, h2, out_norm_g,
                         tm=min(512, T), te=512)


def kernel(x, mem, mix_norm_g, w_in, conv_dw_w, conv_dw_b, conv_ln_g, conv_ln_b, w_conv_out, b_conv_out, w_attn_out, w_mix_out, xattn_norm_g, mem_norm_g, w_xq, w_xkv, w_xo, ffn_norm_g, w_peer_q, peer_subkeys, peer_u, peer_v, final_norm_g):
    depth = w_in.shape[0]
    assert depth == 1, "the last layer's kernel applies the final norm; deeper stacks need a plain-residual variant"
    B, S, D = x.shape
    out = _layer(x, mem, mix_norm_g[0], w_in[0], conv_dw_w[0], conv_dw_b[0], conv_ln_g[0], conv_ln_b[0],
                 w_conv_out[0], b_conv_out[0], w_attn_out[0], w_mix_out[0], xattn_norm_g[0], mem_norm_g[0],
                 w_xq[0], w_xkv[0], w_xo[0], ffn_norm_g[0], w_peer_q[0], peer_subkeys[0], peer_u[0], peer_v[0],
                 final_norm_g)
    return out.reshape(B, S, D)
```
